```python
import jax, jax.numpy as jnp
from jax import lax
import numpy as np

D_MODEL = 1024
BATCH = 8
SEQ = 4096
DEPTH = 1
DEC_BATCH = 16
DEC_SEQ = 64
PAST_LEN = 2048

CHUNK = 64
MIX_WIDTH = D_MODEL
HG_WIDTH = MIX_WIDTH // 2
RG_WIDTH = MIX_WIDTH - HG_WIDTH
HG_HEAD_DIM = 128
HG_HEADS = HG_WIDTH // HG_HEAD_DIM
RG_BLOCKS = 8
RG_BLOCK_DIM = RG_WIDTH // RG_BLOCKS
CONV_WIDTH = 4
RG_C = 8.0
D_FF = 4 * D_MODEL
N_MOD = 6
EPS = 1e-6
IN_WIDTH = 4 * HG_WIDTH + 2 * RG_WIDTH

kernel_name = "hymba_hgrn2_rglru_streaming_step"


def _rms(x):
    xf = x.astype(jnp.float32)
    return (xf * lax.rsqrt(jnp.mean(xf * xf, axis=-1, keepdims=True) + EPS)).astype(x.dtype)


def _hgrn2_chunk(S0, q, logf, k, v):
    L = q.shape[2]
    b = jnp.cumsum(logf, axis=2)
    mask = jnp.tril(jnp.ones((L, L), dtype=bool))
    diff = b[:, :, :, None, :] - b[:, :, None, :, :]
    decay = jnp.exp(jnp.where(mask[:, :, None], diff, -jnp.inf))
    scores = jnp.einsum('bhtd,bhsd,bhtsd->bhts', q, k, decay)
    o = (jnp.einsum('bhts,bhsv->bhtv', scores, v)
         + jnp.einsum('bhtd,bhdv->bhtv', q * jnp.exp(b), S0))
    b_last = b[:, :, -1:, :]
    S = (jnp.exp(b_last[:, :, 0, :])[..., None] * S0
         + jnp.einsum('bhsd,bhsv->bhdv', k * jnp.exp(b_last - b), v))
    return S, o


def _hgrn2_sequence(S0, q, logf, k, v):
    Bn, H, L, DK = q.shape
    if L <= CHUNK:
        return _hgrn2_chunk(S0, q, logf, k, v)
    n = L // CHUNK

    def to_chunks(t):
        return jnp.moveaxis(t.reshape(Bn, H, n, CHUNK, t.shape[-1]), 2, 0)

    def step(S, inp):
        return _hgrn2_chunk(S, *inp)

    S, o = lax.scan(step, S0, (to_chunks(q), to_chunks(logf), to_chunks(k), to_chunks(v)))
    o = jnp.moveaxis(o, 0, 2).reshape(Bn, H, L, v.shape[-1])
    return S, o


def _lin_combine(left, right):
    a1, b1 = left
    a2, b2 = right
    return a1 * a2, a2 * b1 + b2


def _layer(x, c, S0, h0, conv_buf, lb, w_ada, b_ada, w_in, hg_gain, conv_w, conv_b,
           rg_wa, rg_ba, rg_wx, rg_bx, rg_lam, w_out, w_up, w_down):
    f32 = jnp.float32
    Bn, L, _ = x.shape
    mod = jax.nn.silu(c) @ w_ada + b_ada
    sh1, sc1, g1, sh2, sc2, g2 = jnp.split(mod[:, None, :], N_MOD, axis=-1)

    hn = _rms(x) * (1 + sc1) + sh1
    proj = hn @ w_in
    q, f, iv, og, xr, gr = jnp.split(
        proj, [HG_WIDTH, 2 * HG_WIDTH, 3 * HG_WIDTH, 4 * HG_WIDTH, 4 * HG_WIDTH + RG_WIDTH], axis=-1)

    def heads(t):
        return t.reshape(Bn, L, HG_HEADS, HG_HEAD_DIM).transpose(0, 2, 1, 3).astype(f32)

    fgate = lb + (1.0 - lb) * jax.nn.sigmoid(f.astype(f32))
    S_new, o = _hgrn2_sequence(S0.astype(f32), heads(jax.nn.silu(q)), heads(jnp.log(fgate)),
                               heads(1.0 - fgate), heads(iv))
    o = o.transpose(0, 2, 1, 3)
    o = o * lax.rsqrt(jnp.mean(o * o, axis=-1, keepdims=True) + EPS)
    o_hg = o.reshape(Bn, L, HG_WIDTH).astype(x.dtype) * hg_gain * jax.nn.silu(og)

    xpad = jnp.concatenate([conv_buf.astype(xr.dtype), xr], axis=1)
    xc = conv_b + xpad[:, 0:L] * conv_w[0]
    for j in range(1, CONV_WIDTH):
        xc = xc + xpad[:, j:j + L] * conv_w[j]
    new_buf = xpad[:, -(CONV_WIDTH - 1):]
    xb = xc.reshape(Bn, L, RG_BLOCKS, RG_BLOCK_DIM)
    r = jax.nn.sigmoid(jnp.einsum('blnc,ncd->blnd', xb, rg_wa).reshape(Bn, L, RG_WIDTH) + rg_ba)
    ig = jax.nn.sigmoid(jnp.einsum('blnc,ncd->blnd', xb, rg_wx).reshape(Bn, L, RG_WIDTH) + rg_bx)
    log_a = -RG_C * r.astype(f32) * jax.nn.softplus(-rg_lam.astype(f32))
    a = jnp.exp(log_a)
    u = jnp.sqrt(-jnp.expm1(2.0 * log_a)) * (ig * xc).astype(f32)
    A, Bc = lax.associative_scan(_lin_combine, (a, u), axis=1)
    hseq = A * h0.astype(f32)[:, None, :] + Bc
    h_new = hseq[:, -1]
    o_rg = hseq.astype(x.dtype) * jax.nn.gelu(gr)

    y = jnp.concatenate([o_hg, o_rg], axis=-1) @ w_out
    x = x + g1 * y

    hn2 = _rms(x) * (1 + sc2) + sh2
    x = x + g2 * (jnp.square(jax.nn.relu(hn2 @ w_up)) @ w_down)
    return x, S_new.astype(S0.dtype), h_new.astype(h0.dtype), new_buf.astype(conv_buf.dtype)


def _trunk(x, c, S0s, h0s, bufs, hg_lb_logits, w_ada, b_ada, w_in, hg_norm_gain, conv_w, conv_b,
           rg_wa, rg_ba, rg_wx, rg_bx, rg_lambda, w_out, w_up, w_down, final_gain):
    lbs = jnp.cumsum(jax.nn.softmax(hg_lb_logits.astype(jnp.float32), axis=0), axis=0)
    Ss, hs, cbs = [], [], []
    for l in range(DEPTH):
        x, S, h, cb = _layer(x, c, S0s[l], h0s[l], bufs[l], lbs[l], w_ada[l], b_ada[l], w_in[l],
                             hg_norm_gain[l], conv_w[l], conv_b[l], rg_wa[l], rg_ba[l], rg_wx[l],
                             rg_bx[l], rg_lambda[l], w_out[l], w_up[l], w_down[l])
        Ss.append(S)
        hs.append(h)
        cbs.append(cb)
    y = _rms(x) * final_gain
    return y, jnp.stack(Ss), jnp.stack(hs), jnp.stack(cbs)


def setup_inputs(seed: int = 0) -> dict:
    key = jax.random.key(seed)
    ks = jax.random.split(key, 32)
    nrm = lambda k, shape, s: jax.random.normal(k, shape, jnp.float32) * s
    a_c = jax.random.uniform(ks[0], (DEPTH, RG_WIDTH), jnp.float32, 0.9, 0.999)
    a0 = a_c ** (1.0 / RG_C)
    rg_lambda = jnp.log(a0) - jnp.log1p(-a0)
    return {
        "x_prompt": nrm(ks[1], (BATCH, SEQ, D_MODEL), 1.0),
        "x_sample": nrm(ks[2], (DEC_BATCH, DEC_SEQ, D_MODEL), 1.0),
        "c_prompt": nrm(ks[3], (BATCH, D_MODEL), 1.0),
        "c_sample": nrm(ks[4], (DEC_BATCH, D_MODEL), 1.0),
        "state_hgrn": nrm(ks[5], (DEPTH, DEC_BATCH, HG_HEADS, HG_HEAD_DIM, HG_HEAD_DIM), 0.3),
        "state_rglru": nrm(ks[6], (DEPTH, DEC_BATCH, RG_WIDTH), 0.5),
        "cache_conv": nrm(ks[7], (DEPTH, DEC_BATCH, CONV_WIDTH - 1, RG_WIDTH), 1.0),
        "hg_lb_logits": nrm(ks[8], (DEPTH + 1, HG_WIDTH), 0.5),
        "w_ada": nrm(ks[9], (DEPTH, D_MODEL, N_MOD * D_MODEL), D_MODEL ** -0.5),
        "b_ada": nrm(ks[10], (DEPTH, N_MOD * D_MODEL), 0.02),
        "w_in": nrm(ks[11], (DEPTH, D_MODEL, IN_WIDTH), D_MODEL ** -0.5),
        "hg_norm_gain": 1.0 + nrm(ks[12], (DEPTH, HG_WIDTH), 0.02),
        "conv_w": nrm(ks[13], (DEPTH, CONV_WIDTH, RG_WIDTH), CONV_WIDTH ** -0.5),
        "conv_b": nrm(ks[14], (DEPTH, RG_WIDTH), 0.02),
        "rg_wa": nrm(ks[15], (DEPTH, RG_BLOCKS, RG_BLOCK_DIM, RG_BLOCK_DIM), RG_BLOCK_DIM ** -0.5),
        "rg_ba": nrm(ks[16], (DEPTH, RG_WIDTH), 0.02),
        "rg_wx": nrm(ks[17], (DEPTH, RG_BLOCKS, RG_BLOCK_DIM, RG_BLOCK_DIM), RG_BLOCK_DIM ** -0.5),
        "rg_bx": nrm(ks[18], (DEPTH, RG_WIDTH), 0.02),
        "rg_lambda": rg_lambda,
        "w_out": nrm(ks[19], (DEPTH, MIX_WIDTH, D_MODEL), MIX_WIDTH ** -0.5),
        "w_up": nrm(ks[20], (DEPTH, D_MODEL, D_FF), D_MODEL ** -0.5),
        "w_down": nrm(ks[21], (DEPTH, D_FF, D_MODEL), D_FF ** -0.5),
        "final_gain": 1.0 + nrm(ks[22], (D_MODEL,), 0.02),
    }


def reference(x_prompt, x_sample, c_prompt, c_sample, state_hgrn, state_rglru, cache_conv,
              hg_lb_logits, w_ada, b_ada, w_in, hg_norm_gain, conv_w, conv_b,
              rg_wa, rg_ba, rg_wx, rg_bx, rg_lambda, w_out, w_up, w_down, final_gain):
    Bp = x_prompt.shape[0]
    S0p = jnp.zeros((DEPTH, Bp, HG_HEADS, HG_HEAD_DIM, HG_HEAD_DIM), state_hgrn.dtype)
    h0p = jnp.zeros((DEPTH, Bp, RG_WIDTH), state_rglru.dtype)
    cbp = jnp.zeros((DEPTH, Bp, CONV_WIDTH - 1, RG_WIDTH), cache_conv.dtype)
    y_prompt, S_p, h_p, cb_p = _trunk(x_prompt, c_prompt, S0p, h0p, cbp, hg_lb_logits, w_ada, b_ada,
                                      w_in, hg_norm_gain, conv_w, conv_b, rg_wa, rg_ba, rg_wx, rg_bx,
                                      rg_lambda, w_out, w_up, w_down, final_gain)
    y_sample, S_s, h_s, cb_s = _trunk(x_sample, c_sample, state_hgrn, state_rglru, cache_conv,
                                      hg_lb_logits, w_ada, b_ada, w_in, hg_norm_gain, conv_w, conv_b,
                                      rg_wa, rg_ba, rg_wx, rg_bx, rg_lambda, w_out, w_up, w_down,
                                      final_gain)
    return (y_prompt, y_sample, S_p, h_p, cb_p, S_s, h_s, cb_s)
```

```python
import functools

import jax
import jax.numpy as jnp
from jax import lax
from jax.experimental import pallas as pl
from jax.experimental.pallas import tpu as pltpu

F32 = jnp.float32
BF16 = jnp.bfloat16

D_MODEL = 1024
HG_WIDTH = 512
RG_WIDTH = 512
HEAD_DIM = 128
N_HEADS = 4
N_MOD = 6
D_FF = 4096
IN_WIDTH = 4 * HG_WIDTH + 2 * RG_WIDTH
CONV_WIDTH = 4
RG_C = 8.0
EPS = 1e-6

SUBLANES = 8
CHUNK = 64
N_GROUPS = CHUNK // SUBLANES
VMEM_LIMIT = 52 * 1024 * 1024


def _sigmoid(x):
    return 0.5 * jnp.tanh(0.5 * x) + 0.5


def _silu(x):
    return x * _sigmoid(x)


def _gelu_tanh(x):
    c = 0.7978845608028654
    return 0.5 * x * (1.0 + jnp.tanh(c * (x + 0.044715 * (x * x * x))))


def _rms(x):
    return x * lax.rsqrt(jnp.mean(x * x, axis=-1, keepdims=True) + EPS)


def _dot(a, b):
    return jnp.dot(a, b, preferred_element_type=F32)


def _dot_nt(a, b):
    return lax.dot_general(a, b, (((1,), (1,)), ((), ())), preferred_element_type=F32)


def _ada_kernel(c_ref, w_ref, b_ref, o_ref):
    c = c_ref[...]
    o_ref[...] = _dot(_silu(c).astype(BF16), w_ref[...].astype(BF16)) + b_ref[...]


def _ada(c_all, w_ada, b_ada):
    rows = c_all.shape[0]
    tn = 1536
    return pl.pallas_call(
        _ada_kernel,
        grid=(N_MOD * D_MODEL // tn,),
        in_specs=[
            pl.BlockSpec((rows, D_MODEL), lambda j: (0, 0)),
            pl.BlockSpec((D_MODEL, tn), lambda j: (0, j)),
            pl.BlockSpec((1, tn), lambda j: (0, j)),
        ],
        out_specs=pl.BlockSpec((rows, tn), lambda j: (0, j)),
        out_shape=jax.ShapeDtypeStruct((rows, N_MOD * D_MODEL), F32),
        compiler_params=pltpu.CompilerParams(
            dimension_semantics=("arbitrary",), vmem_limit_bytes=VMEM_LIMIT),
        name="ada_mod",
    )(c_all, w_ada, b_ada)


def _group_scan_products(f3, r8):
    pf = f3
    for sh in (1, 2, 4):
        pf = jnp.where(r8 >= sh, pf * pltpu.roll(pf, sh, 1), pf)
    pb = jnp.where(r8 < SUBLANES - 1, pltpu.roll(f3, SUBLANES - 1, 1), 1.0)
    for sh in (1, 2, 4):
        pb = jnp.where(r8 < SUBLANES - sh, pb * pltpu.roll(pb, SUBLANES - sh, 1), pb)
    return pf, pb


def _block_products(pf8, pb8):
    out = {SUBLANES: (pf8, pb8)}
    curf, curb = pf8, pb8
    h = SUBLANES
    while h < CHUNK:
        g = h // SUBLANES
        nf, nb = [], []
        for grp in range(N_GROUPS):
            base = (grp // (2 * g)) * 2 * g
            if grp % (2 * g) >= g:
                total_lower = curf[base + g - 1][SUBLANES - 1:SUBLANES, :]
                nf.append(curf[grp] * total_lower)
                nb.append(curb[grp])
            else:
                total_upper = curf[base + 2 * g - 1][SUBLANES - 1:SUBLANES, :]
                nf.append(curf[grp])
                nb.append(curb[grp] * total_upper)
        curf, curb = nf, nb
        h *= 2
        out[h] = (curf, curb)
    return out


def _hgrn2_chunk(q, fg, v, st, r8, level_masks):
    k = 1.0 - fg
    q3 = q.reshape(N_GROUPS, SUBLANES, HEAD_DIM)
    f3 = fg.reshape(N_GROUPS, SUBLANES, HEAD_DIM)
    k3 = k.reshape(N_GROUPS, SUBLANES, HEAD_DIM)
    v3 = v.reshape(N_GROUPS, SUBLANES, HEAD_DIM)

    s0 = jnp.sum(q3 * k3, axis=-1, keepdims=True)
    o3 = s0 * v3
    dec = f3
    for delta in range(1, SUBLANES):
        if delta > 1:
            dec = dec * pltpu.roll(f3, delta - 1, 1)
        kd = pltpu.roll(k3, delta, 1)
        vd = pltpu.roll(v3, delta, 1)
        s = jnp.sum(q3 * dec * kd, axis=-1, keepdims=True)
        o3 = o3 + jnp.where(r8 >= delta, s, 0.0) * vd
    o = o3.reshape(CHUNK, HEAD_DIM)

    pf8, pb8 = _group_scan_products(f3, r8)
    prods = _block_products([pf8[g] for g in range(N_GROUPS)],
                            [pb8[g] for g in range(N_GROUPS)])
    qg = [q3[g] for g in range(N_GROUPS)]
    kg = [k3[g] for g in range(N_GROUPS)]

    def scaled(tiles, facs):
        return jnp.concatenate([t * p for t, p in zip(tiles, facs)], axis=0).astype(BF16)

    scores = None
    h = SUBLANES
    while h < CHUNK:
        a = _dot_nt(scaled(qg, prods[h][0]), scaled(kg, prods[h][1]))
        a = jnp.where(level_masks[h], a, 0.0)
        scores = a if scores is None else scores + a
        h *= 2
    vb = v.astype(BF16)
    o = o + _dot(scores.astype(BF16), vb)

    o = o + _dot_nt(scaled(qg, prods[CHUNK][0]), st.astype(BF16))
    total = prods[CHUNK][0][N_GROUPS - 1][SUBLANES - 1:SUBLANES, :]
    st_new = st * total + _dot(v.T.astype(BF16), scaled(kg, prods[CHUNK][1]))
    return o, st_new


def _mixer_kernel(x_ref, mod_ref, s0_ref, h0_ref, cb0_ref, lbl_ref, win_ref, gain_ref,
                  cw_ref, cbias_ref, wrg_ref, brg_ref, lam_ref, wout_ref,
                  x1_ref, s_ref, h_ref, cbo_ref,
                  hn_s, proj_s, xr_s, gate_s, xc_s, omix_s, st_s, hc_s, *, nb, t):
    i = pl.program_id(1)
    last = pl.num_programs(1) - 1
    tail = CONV_WIDTH - 1
    pad = SUBLANES

    @pl.when(i == 0)
    def _():
        for n in range(nb):
            for h in range(N_HEADS):
                st_s[n, h] = s0_ref[n, h].T
            hc_s[n] = h0_ref[n]
            xr_s[n, 0:pad, :] = jnp.zeros((pad, RG_WIDTH), F32)
            xr_s[n, pad - tail:pad, :] = cb0_ref[n]

    for n in range(nb):
        m = mod_ref[n]
        sh1 = m[:, 0:D_MODEL]
        sc1 = m[:, D_MODEL:2 * D_MODEL]
        hn = _rms(x_ref[n]) * (1.0 + sc1) + sh1
        hn_s[n * t:(n + 1) * t, :] = hn.astype(BF16)
    proj_s[...] = _dot(hn_s[...], win_ref[...])

    cw = cw_ref[...]
    for n in range(nb):
        xr_s[n, pad:pad + t, :] = proj_s[n * t:(n + 1) * t, 4 * HG_WIDTH:4 * HG_WIDTH + RG_WIDTH]
        xc = cbias_ref[...] + xr_s[n, pad - tail:pad - tail + t, :] * cw[0:1, :]
        for j in range(1, CONV_WIDTH):
            xc = xc + xr_s[n, pad - tail + j:pad - tail + j + t, :] * cw[j:j + 1, :]
        xc_s[n * t:(n + 1) * t, :] = xc
    gate_s[...] = _dot(xc_s[...].astype(BF16), wrg_ref[...]) + brg_ref[...]

    lbl = lbl_ref[...]
    lmax = jnp.maximum(lbl[0:1, :], lbl[1:2, :])
    e0 = jnp.exp(lbl[0:1, :] - lmax)
    e1 = jnp.exp(lbl[1:2, :] - lmax)
    lb = e0 / (e0 + e1)
    nlam = -lam_ref[...]
    ez = jnp.exp(-jnp.abs(nlam))
    one_p = 1.0 + ez
    log1p_ez = jnp.where(one_p == 1.0, ez, jnp.log(one_p) * (ez / (one_p - 1.0)))
    softplus_nlam = jnp.maximum(nlam, 0.0) + log1p_ez
    gain = gain_ref[...]

    r8 = lax.broadcasted_iota(jnp.int32, (N_GROUPS, SUBLANES, HEAD_DIM), 1)
    r8w = lax.broadcasted_iota(jnp.int32, (N_GROUPS, SUBLANES, RG_WIDTH), 1)
    ti = lax.broadcasted_iota(jnp.int32, (CHUNK, CHUNK), 0)
    si = lax.broadcasted_iota(jnp.int32, (CHUNK, CHUNK), 1)
    level_masks = {}
    h = SUBLANES
    while h < CHUNK:
        shift = h.bit_length() - 1
        level_masks[h] = (((ti ^ si) >> shift) == 1) & (ti > si)
        h *= 2

    for n in range(nb):
        def chunk_body(c, carry, n=n):
            r0 = pl.multiple_of(n * t + c * CHUNK, CHUNK)
            rows = pl.ds(r0, CHUNK)
            for hd in range(N_HEADS):
                c0 = hd * HEAD_DIM
                cols = slice(c0, c0 + HEAD_DIM)
                q = _silu(proj_s[rows, c0:c0 + HEAD_DIM])
                lb_h = lb[:, cols]
                fg = lb_h + (1.0 - lb_h) * _sigmoid(
                    proj_s[rows, HG_WIDTH + c0:HG_WIDTH + c0 + HEAD_DIM])
                v = proj_s[rows, 2 * HG_WIDTH + c0:2 * HG_WIDTH + c0 + HEAD_DIM]
                og = proj_s[rows, 3 * HG_WIDTH + c0:3 * HG_WIDTH + c0 + HEAD_DIM]
                o, st_new = _hgrn2_chunk(q, fg, v, st_s[n, hd], r8, level_masks)
                st_s[n, hd] = st_new
                o = _rms(o) * gain[:, cols] * _silu(og)
                omix_s[rows, c0:c0 + HEAD_DIM] = o.astype(BF16)
            r = _sigmoid(gate_s[rows, 0:RG_WIDTH])
            ig = _sigmoid(gate_s[rows, RG_WIDTH:2 * RG_WIDTH])
            xc = xc_s[rows, :]
            log_a = (-RG_C) * r * softplus_nlam
            a = jnp.exp(log_a)
            th = jnp.tanh(log_a)
            u = jnp.sqrt(-2.0 * th / (1.0 - th)) * (ig * xc)
            ca = a.reshape(N_GROUPS, SUBLANES, RG_WIDTH)
            cu = u.reshape(N_GROUPS, SUBLANES, RG_WIDTH)
            for sh in (1, 2, 4):
                keep = r8w >= sh
                cu = jnp.where(keep, ca * pltpu.roll(cu, sh, 1) + cu, cu)
                ca = jnp.where(keep, ca * pltpu.roll(ca, sh, 1), ca)
            hprev = hc_s[n]
            hs = []
            for g in range(N_GROUPS):
                hg = ca[g] * hprev + cu[g]
                hs.append(hg)
                hprev = hg[SUBLANES - 1:SUBLANES, :]
            hc_s[n] = hprev
            hseq = jnp.concatenate(hs, axis=0)
            gr = proj_s[rows, 4 * HG_WIDTH + RG_WIDTH:IN_WIDTH]
            omix_s[rows, HG_WIDTH:HG_WIDTH + RG_WIDTH] = (hseq * _gelu_tanh(gr)).astype(BF16)
            return carry

        lax.fori_loop(0, t // CHUNK, chunk_body, 0)

    y = _dot(omix_s[...], wout_ref[...])
    for n in range(nb):
        g1 = mod_ref[n][:, 2 * D_MODEL:3 * D_MODEL]
        x1_ref[n] = x_ref[n] + g1 * y[n * t:(n + 1) * t, :]

    @pl.when(i == last)
    def _():
        for n in range(nb):
            for h in range(N_HEADS):
                s_ref[n, h] = st_s[n, h].T
            h_ref[n] = hc_s[n]
            cbo_ref[n] = xr_s[n, pad + t - tail:pad + t, :]

    for n in range(nb):
        xr_s[n, pad - tail:pad, :] = xr_s[n, pad + t - tail:pad + t, :]


def _const_spec(shape):
    zeros = (0,) * len(shape)
    return pl.BlockSpec(shape, lambda b, i: zeros, pipeline_mode=pl.Buffered(1))


def _mixer(x, mod, s0, h0, cb0, lbl, w_in, gain, conv_w, conv_b, w_rg, b_rg, lam, w_out, *, nb, t):
    bsz, seq, _ = x.shape
    grid = (bsz // nb, seq // t)
    rows = nb * t
    kern = functools.partial(_mixer_kernel, nb=nb, t=t)
    seq_spec = pl.BlockSpec((nb, t, D_MODEL), lambda b, i: (b, i, 0))
    in_specs = [
        seq_spec,
        pl.BlockSpec((nb, 1, N_MOD * D_MODEL), lambda b, i: (b, 0, 0)),
        pl.BlockSpec((nb, N_HEADS, HEAD_DIM, HEAD_DIM), lambda b, i: (b, 0, 0, 0)),
        pl.BlockSpec((nb, 1, RG_WIDTH), lambda b, i: (b, 0, 0)),
        pl.BlockSpec((nb, CONV_WIDTH - 1, RG_WIDTH), lambda b, i: (b, 0, 0)),
        _const_spec((2, HG_WIDTH)),
        _const_spec((D_MODEL, IN_WIDTH)),
        _const_spec((1, HG_WIDTH)),
        _const_spec((CONV_WIDTH, RG_WIDTH)),
        _const_spec((1, RG_WIDTH)),
        _const_spec((RG_WIDTH, 2 * RG_WIDTH)),
        _const_spec((1, 2 * RG_WIDTH)),
        _const_spec((1, RG_WIDTH)),
        _const_spec((D_MODEL, D_MODEL)),
    ]
    out_specs = [
        seq_spec,
        pl.BlockSpec((nb, N_HEADS, HEAD_DIM, HEAD_DIM), lambda b, i: (b, 0, 0, 0)),
        pl.BlockSpec((nb, 1, RG_WIDTH), lambda b, i: (b, 0, 0)),
        pl.BlockSpec((nb, CONV_WIDTH - 1, RG_WIDTH), lambda b, i: (b, 0, 0)),
    ]
    out_shape = [
        jax.ShapeDtypeStruct((bsz, seq, D_MODEL), F32),
        jax.ShapeDtypeStruct((bsz, N_HEADS, HEAD_DIM, HEAD_DIM), F32),
        jax.ShapeDtypeStruct((bsz, 1, RG_WIDTH), F32),
        jax.ShapeDtypeStruct((bsz, CONV_WIDTH - 1, RG_WIDTH), F32),
    ]
    scratch = [
        pltpu.VMEM((rows, D_MODEL), BF16),
        pltpu.VMEM((rows, IN_WIDTH), F32),
        pltpu.VMEM((nb, SUBLANES + t, RG_WIDTH), F32),
        pltpu.VMEM((rows, 2 * RG_WIDTH), F32),
        pltpu.VMEM((rows, RG_WIDTH), F32),
        pltpu.VMEM((rows, D_MODEL), BF16),
        pltpu.VMEM((nb, N_HEADS, HEAD_DIM, HEAD_DIM), F32),
        pltpu.VMEM((nb, 1, RG_WIDTH), F32),
    ]
    return pl.pallas_call(
        kern,
        grid=grid,
        in_specs=in_specs,
        out_specs=out_specs,
        out_shape=out_shape,
        scratch_shapes=scratch,
        compiler_params=pltpu.CompilerParams(
            dimension_semantics=("arbitrary", "arbitrary"), vmem_limit_bytes=VMEM_LIMIT),
        name="mixer",
    )(x, mod, s0, h0, cb0, lbl, w_in, gain, conv_w, conv_b, w_rg, b_rg, lam, w_out)


def _mlp_kernel(x1_ref, mod_ref, wup_ref, wdn_ref, fgain_ref, y_ref, hn_s, *, nb, t):
    for n in range(nb):
        m = mod_ref[n]
        sh2 = m[:, 3 * D_MODEL:4 * D_MODEL]
        sc2 = m[:, 4 * D_MODEL:5 * D_MODEL]
        hn_s[n * t:(n + 1) * t, :] = (_rms(x1_ref[n]) * (1.0 + sc2) + sh2).astype(BF16)
    hn = hn_s[...]
    acc = None
    for j in range(D_FF // D_MODEL):
        up = _dot(hn, wup_ref[:, j * D_MODEL:(j + 1) * D_MODEL])
        act = jnp.square(jnp.maximum(up, 0.0)).astype(BF16)
        dn = _dot(act, wdn_ref[j * D_MODEL:(j + 1) * D_MODEL, :])
        acc = dn if acc is None else acc + dn
    for n in range(nb):
        g2 = mod_ref[n][:, 5 * D_MODEL:6 * D_MODEL]
        x2 = x1_ref[n] + g2 * acc[n * t:(n + 1) * t, :]
        y_ref[n] = _rms(x2) * fgain_ref[...]


def _mlp(x1, mod, w_up, w_down, fgain, *, nb, t):
    bsz, seq, _ = x1.shape
    grid = (bsz // nb, seq // t)
    kern = functools.partial(_mlp_kernel, nb=nb, t=t)
    seq_spec = pl.BlockSpec((nb, t, D_MODEL), lambda b, i: (b, i, 0))
    return pl.pallas_call(
        kern,
        grid=grid,
        in_specs=[
            seq_spec,
            pl.BlockSpec((nb, 1, N_MOD * D_MODEL), lambda b, i: (b, 0, 0)),
            _const_spec((D_MODEL, D_FF)),
            _const_spec((D_FF, D_MODEL)),
            _const_spec((1, D_MODEL)),
        ],
        out_specs=seq_spec,
        out_shape=jax.ShapeDtypeStruct((bsz, seq, D_MODEL), F32),
        scratch_shapes=[pltpu.VMEM((nb * t, D_MODEL), BF16)],
        compiler_params=pltpu.CompilerParams(
            dimension_semantics=("arbitrary", "arbitrary"), vmem_limit_bytes=VMEM_LIMIT),
        name="mlp",
    )(x1, mod, w_up, w_down, fgain)


def _block_diag(w):
    n, c, d = w.shape
    eye = jnp.eye(n, dtype=w.dtype)
    return (eye[:, None, :, None] * w[:, :, None, :]).reshape(n * c, n * d)


def kernel(x_prompt, x_sample, c_prompt, c_sample, state_hgrn, state_rglru, cache_conv,
           hg_lb_logits, w_ada, b_ada, w_in, hg_norm_gain, conv_w, conv_b,
           rg_wa, rg_ba, rg_wx, rg_bx, rg_lambda, w_out, w_up, w_down, final_gain):
    bp = x_prompt.shape[0]
    bs = x_sample.shape[0]

    w_in_b = w_in[0].astype(BF16)
    w_out_b = w_out[0].astype(BF16)
    w_up_b = w_up[0].astype(BF16)
    w_down_b = w_down[0].astype(BF16)
    w_rg = jnp.concatenate([_block_diag(rg_wa[0]), _block_diag(rg_wx[0])], axis=1).astype(BF16)
    b_rg = jnp.concatenate([rg_ba[0], rg_bx[0]])[None, :]
    gain = hg_norm_gain[0][None, :]
    cbias = conv_b[0][None, :]
    lam = rg_lambda[0][None, :]
    fgain = final_gain[None, :]

    c_all = jnp.concatenate([c_prompt, c_sample], axis=0)
    mod = _ada(c_all, w_ada[0], b_ada[0][None, :])
    mod_p = mod[:bp][:, None, :]
    mod_s = mod[bp:][:, None, :]

    def trunk(x, mod_x, s0, h0, cb0, nb, t):
        x1, s_new, h_new, cb_new = _mixer(
            x, mod_x, s0, h0[:, None, :], cb0, hg_lb_logits, w_in_b, gain, conv_w[0], cbias,
            w_rg, b_rg, lam, w_out_b, nb=nb, t=t)
        y = _mlp(x1, mod_x, w_up_b, w_down_b, fgain, nb=nb, t=t)
        return y, s_new[None], h_new[:, 0, :][None], cb_new[None]

    s0p = jnp.zeros((bp, N_HEADS, HEAD_DIM, HEAD_DIM), state_hgrn.dtype)
    h0p = jnp.zeros((bp, RG_WIDTH), state_rglru.dtype)
    cbp = jnp.zeros((bp, CONV_WIDTH - 1, RG_WIDTH), cache_conv.dtype)
    y_p, s_p, h_p, cb_p = trunk(x_prompt, mod_p, s0p, h0p, cbp, 1, 256)
    y_s, s_s, h_s, cb_s = trunk(x_sample, mod_s, state_hgrn[0], state_rglru[0], cache_conv[0],
                                1, x_sample.shape[1])
    return (y_p, y_s, s_p, h_p, cb_p, s_s, h_s, cb_s)
```

```python
import functools

import jax
import jax.numpy as jnp
from jax import lax
from jax.experimental import pallas as pl
from jax.experimental.pallas import tpu as pltpu

F32 = jnp.float32
BF16 = jnp.bfloat16

D_MODEL = 1024
HG_WIDTH = 512
RG_WIDTH = 512
HEAD_DIM = 128
N_HEADS = 4
N_MOD = 6
D_FF = 4096
IN_WIDTH = 4 * HG_WIDTH + 2 * RG_WIDTH
CONV_WIDTH = 4
RG_C = 8.0
EPS = 1e-6

SUBLANES = 8
CHUNK = 64
N_GROUPS = CHUNK // SUBLANES
VMEM_LIMIT = 52 * 1024 * 1024


def _sigmoid(x):
    return 0.5 * jnp.tanh(0.5 * x) + 0.5


def _silu(x):
    h = 0.5 * x
    return h + h * jnp.tanh(h)


def _gelu_tanh(x):
    c = 0.7978845608028654
    return 0.5 * x * (1.0 + jnp.tanh(c * (x + 0.044715 * (x * x * x))))


def _rms(x):
    return x * lax.rsqrt(jnp.mean(x * x, axis=-1, keepdims=True) + EPS)


def _dot(a, b):
    return jnp.dot(a, b, preferred_element_type=F32)


def _dot_nt(a, b):
    return lax.dot_general(a, b, (((1,), (1,)), ((), ())), preferred_element_type=F32)


def _ada_kernel(c_ref, w_ref, b_ref, o_ref):
    c = c_ref[...]
    o_ref[...] = _dot(_silu(c).astype(BF16), w_ref[...].astype(BF16)) + b_ref[...]


def _ada(c_all, w_ada, b_ada):
    rows = c_all.shape[0]
    tn = 1536
    return pl.pallas_call(
        _ada_kernel,
        grid=(N_MOD * D_MODEL // tn,),
        in_specs=[
            pl.BlockSpec((rows, D_MODEL), lambda j: (0, 0)),
            pl.BlockSpec((D_MODEL, tn), lambda j: (0, j)),
            pl.BlockSpec((1, tn), lambda j: (0, j)),
        ],
        out_specs=pl.BlockSpec((rows, tn), lambda j: (0, j)),
        out_shape=jax.ShapeDtypeStruct((rows, N_MOD * D_MODEL), F32),
        compiler_params=pltpu.CompilerParams(
            dimension_semantics=("arbitrary",), vmem_limit_bytes=VMEM_LIMIT),
        name="ada_mod",
    )(c_all, w_ada, b_ada)


def _group_scan_products(f3, r8):
    pf = f3
    for sh in (1, 2, 4):
        pf = jnp.where(r8 >= sh, pf * pltpu.roll(pf, sh, 1), pf)
    pb = jnp.where(r8 < SUBLANES - 1, pltpu.roll(f3, SUBLANES - 1, 1), 1.0)
    for sh in (1, 2, 4):
        pb = jnp.where(r8 < SUBLANES - sh, pb * pltpu.roll(pb, SUBLANES - sh, 1), pb)
    return pf, pb


def _block_products(pf8, pb8):
    out = {SUBLANES: (pf8, pb8)}
    curf, curb = pf8, pb8
    h = SUBLANES
    while h < CHUNK:
        g = h // SUBLANES
        nf, nb = [], []
        for grp in range(N_GROUPS):
            base = (grp // (2 * g)) * 2 * g
            if grp % (2 * g) >= g:
                total_lower = curf[base + g - 1][SUBLANES - 1:SUBLANES, :]
                nf.append(curf[grp] * total_lower)
                nb.append(curb[grp])
            else:
                total_upper = curf[base + 2 * g - 1][SUBLANES - 1:SUBLANES, :]
                nf.append(curf[grp])
                nb.append(curb[grp] * total_upper)
        curf, curb = nf, nb
        h *= 2
        out[h] = (curf, curb)
    return out


def _hgrn2_chunk(q, fg, k, v, st, r8, band_diff, level_masks):
    q3 = q.reshape(N_GROUPS, SUBLANES, HEAD_DIM)
    f3 = fg.reshape(N_GROUPS, SUBLANES, HEAD_DIM)
    k3 = k.reshape(N_GROUPS, SUBLANES, HEAD_DIM)

    s = jnp.sum(q3 * k3, axis=-1, keepdims=True).reshape(CHUNK, 1)
    scores = jnp.where(band_diff == 0, s, 0.0)
    kd = k3
    for delta in range(1, SUBLANES):
        kd = f3 * pltpu.roll(kd, 1, 1)
        s = jnp.sum(q3 * kd, axis=-1, keepdims=True).reshape(CHUNK, 1)
        scores = jnp.where(band_diff == delta, s, scores)

    pf8, pb8 = _group_scan_products(f3, r8)
    prods = _block_products([pf8[g] for g in range(N_GROUPS)],
                            [pb8[g] for g in range(N_GROUPS)])
    qg = [q3[g] for g in range(N_GROUPS)]
    kg = [k3[g] for g in range(N_GROUPS)]

    def scaled(tiles, facs):
        return jnp.concatenate([t * p for t, p in zip(tiles, facs)], axis=0).astype(BF16)

    h = SUBLANES
    while h < CHUNK:
        a = _dot_nt(scaled(qg, prods[h][0]), scaled(kg, prods[h][1]))
        scores = jnp.where(level_masks[h], a, scores)
        h *= 2
    o = _dot(scores.astype(BF16), v.astype(BF16))

    o = o + _dot_nt(scaled(qg, prods[CHUNK][0]), st.astype(BF16))
    total = prods[CHUNK][0][N_GROUPS - 1][SUBLANES - 1:SUBLANES, :]
    st_new = st * total + _dot(v.T.astype(BF16), scaled(kg, prods[CHUNK][1]))
    return o, st_new


def _mixer_kernel(x_ref, mod_ref, s0_ref, h0_ref, cb0_ref, lbl_ref, win_ref, gain_ref,
                  cw_ref, cbias_ref, wrg_ref, brg_ref, lam_ref, wout_ref,
                  x1_ref, s_ref, h_ref, cbo_ref,
                  hn_s, proj_s, xr_s, gate_s, xc_s, omix_s, st_s, hc_s,
                  win_s, wrg_s, wout_s, *, nb, t):
    i = pl.program_id(1)
    last = pl.num_programs(1) - 1
    tail = CONV_WIDTH - 1
    pad = SUBLANES

    @pl.when((pl.program_id(0) == 0) & (i == 0))
    def _():
        win_s[...] = win_ref[...]
        wrg_s[...] = wrg_ref[...]
        wout_s[...] = wout_ref[...]

    @pl.when(i == 0)
    def _():
        for n in range(nb):
            for h in range(N_HEADS):
                st_s[n, h] = s0_ref[n, h].T
            hc_s[n] = h0_ref[n]
            xr_s[n, 0:pad, :] = jnp.zeros((pad, RG_WIDTH), F32)
            xr_s[n, pad - tail:pad, :] = cb0_ref[n]

    for n in range(nb):
        m = mod_ref[n]
        sh1 = m[:, 0:D_MODEL]
        sc1 = m[:, D_MODEL:2 * D_MODEL]
        hn = _rms(x_ref[n]) * (1.0 + sc1) + sh1
        hn_s[n * t:(n + 1) * t, :] = hn.astype(BF16)
    proj_s[...] = _dot(hn_s[...], win_s[...])

    cw = cw_ref[...]
    for n in range(nb):
        xr_s[n, pad:pad + t, :] = proj_s[n * t:(n + 1) * t, 4 * HG_WIDTH:4 * HG_WIDTH + RG_WIDTH]
        xc = cbias_ref[...] + xr_s[n, pad - tail:pad - tail + t, :] * cw[0:1, :]
        for j in range(1, CONV_WIDTH):
            xc = xc + xr_s[n, pad - tail + j:pad - tail + j + t, :] * cw[j:j + 1, :]
        xc_s[n * t:(n + 1) * t, :] = xc
    gate_s[...] = _dot(xc_s[...].astype(BF16), wrg_s[...]) + brg_ref[...]

    lbl = lbl_ref[...]
    lmax = jnp.maximum(lbl[0:1, :], lbl[1:2, :])
    e0 = jnp.exp(lbl[0:1, :] - lmax)
    e1 = jnp.exp(lbl[1:2, :] - lmax)
    lb = e0 / (e0 + e1)
    fg_mid = 0.5 * (1.0 + lb)
    fg_half = 0.5 * (1.0 - lb)
    nlam = -lam_ref[...]
    ez = jnp.exp(-jnp.abs(nlam))
    one_p = 1.0 + ez
    log1p_ez = jnp.where(one_p == 1.0, ez, jnp.log(one_p) * (ez / (one_p - 1.0)))
    softplus_nlam = jnp.maximum(nlam, 0.0) + log1p_ez
    quarter_rate = (-0.25 * RG_C) * softplus_nlam
    gain = gain_ref[...]

    r8 = lax.broadcasted_iota(jnp.int32, (N_GROUPS, SUBLANES, HEAD_DIM), 1)
    r8w = lax.broadcasted_iota(jnp.int32, (N_GROUPS, SUBLANES, RG_WIDTH), 1)
    ti = lax.broadcasted_iota(jnp.int32, (CHUNK, CHUNK), 0)
    si = lax.broadcasted_iota(jnp.int32, (CHUNK, CHUNK), 1)
    grp = SUBLANES.bit_length() - 1
    band_diff = jnp.where((ti >> grp) == (si >> grp), ti - si, -1)
    level_masks = {}
    h = SUBLANES
    while h < CHUNK:
        shift = h.bit_length() - 1
        level_masks[h] = (((ti ^ si) >> shift) == 1) & (ti > si)
        h *= 2

    for n in range(nb):
        def chunk_body(c, carry, n=n):
            r0 = pl.multiple_of(n * t + c * CHUNK, CHUNK)
            rows = pl.ds(r0, CHUNK)
            for hd in range(N_HEADS):
                c0 = hd * HEAD_DIM
                cols = slice(c0, c0 + HEAD_DIM)
                q = _silu(proj_s[rows, c0:c0 + HEAD_DIM])
                kt = fg_half[:, cols] * jnp.tanh(
                    0.5 * proj_s[rows, HG_WIDTH + c0:HG_WIDTH + c0 + HEAD_DIM])
                fg = fg_mid[:, cols] + kt
                k = fg_half[:, cols] - kt
                v = proj_s[rows, 2 * HG_WIDTH + c0:2 * HG_WIDTH + c0 + HEAD_DIM]
                og = proj_s[rows, 3 * HG_WIDTH + c0:3 * HG_WIDTH + c0 + HEAD_DIM]
                o, st_new = _hgrn2_chunk(q, fg, k, v, st_s[n, hd], r8, band_diff, level_masks)
                st_s[n, hd] = st_new
                o = _rms(o) * gain[:, cols] * _silu(og)
                omix_s[rows, c0:c0 + HEAD_DIM] = o.astype(BF16)
            xc = xc_s[rows, :]
            half_z = (1.0 + jnp.tanh(0.5 * gate_s[rows, 0:RG_WIDTH])) * quarter_rate
            tp = jnp.tanh(half_z)
            w = 1.0 / (1.0 - tp)
            a = (1.0 + tp) * w
            ig2 = 1.0 + jnp.tanh(0.5 * gate_s[rows, RG_WIDTH:2 * RG_WIDTH])
            u = (jnp.sqrt(-tp) * w) * (ig2 * xc)
            ca = a.reshape(N_GROUPS, SUBLANES, RG_WIDTH)
            cu = u.reshape(N_GROUPS, SUBLANES, RG_WIDTH)
            for sh in (1, 2, 4):
                keep = r8w >= sh
                cu = jnp.where(keep, ca * pltpu.roll(cu, sh, 1) + cu, cu)
                ca = jnp.where(keep, ca * pltpu.roll(ca, sh, 1), ca)
            hprev = hc_s[n]
            hs = []
            for g in range(N_GROUPS):
                hg = ca[g] * hprev + cu[g]
                hs.append(hg)
                hprev = hg[SUBLANES - 1:SUBLANES, :]
            hc_s[n] = hprev
            hseq = jnp.concatenate(hs, axis=0)
            gr = proj_s[rows, 4 * HG_WIDTH + RG_WIDTH:IN_WIDTH]
            omix_s[rows, HG_WIDTH:HG_WIDTH + RG_WIDTH] = (hseq * _gelu_tanh(gr)).astype(BF16)
            return carry

        lax.fori_loop(0, t // CHUNK, chunk_body, 0, unroll=min(2, t // CHUNK))

    y = _dot(omix_s[...], wout_s[...])
    for n in range(nb):
        g1 = mod_ref[n][:, 2 * D_MODEL:3 * D_MODEL]
        x1_ref[n] = x_ref[n] + g1 * y[n * t:(n + 1) * t, :]

    @pl.when(i == last)
    def _():
        for n in range(nb):
            for h in range(N_HEADS):
                s_ref[n, h] = st_s[n, h].T
            h_ref[n] = hc_s[n]
            cbo_ref[n] = xr_s[n, pad + t - tail:pad + t, :]

    for n in range(nb):
        xr_s[n, pad - tail:pad, :] = xr_s[n, pad + t - tail:pad + t, :]


def _const_spec(shape):
    zeros = (0,) * len(shape)
    return pl.BlockSpec(shape, lambda b, i: zeros, pipeline_mode=pl.Buffered(1))


def _mixer(x, mod, s0, h0, cb0, lbl, w_in, gain, conv_w, conv_b, w_rg, b_rg, lam, w_out, *, nb, t):
    bsz, seq, _ = x.shape
    grid = (bsz // nb, seq // t)
    rows = nb * t
    kern = functools.partial(_mixer_kernel, nb=nb, t=t)
    seq_spec = pl.BlockSpec((nb, t, D_MODEL), lambda b, i: (b, i, 0))
    in_specs = [
        seq_spec,
        pl.BlockSpec((nb, 1, N_MOD * D_MODEL), lambda b, i: (b, 0, 0)),
        pl.BlockSpec((nb, N_HEADS, HEAD_DIM, HEAD_DIM), lambda b, i: (b, 0, 0, 0)),
        pl.BlockSpec((nb, 1, RG_WIDTH), lambda b, i: (b, 0, 0)),
        pl.BlockSpec((nb, CONV_WIDTH - 1, RG_WIDTH), lambda b, i: (b, 0, 0)),
        _const_spec((2, HG_WIDTH)),
        _const_spec((D_MODEL, IN_WIDTH)),
        _const_spec((1, HG_WIDTH)),
        _const_spec((CONV_WIDTH, RG_WIDTH)),
        _const_spec((1, RG_WIDTH)),
        _const_spec((RG_WIDTH, 2 * RG_WIDTH)),
        _const_spec((1, 2 * RG_WIDTH)),
        _const_spec((1, RG_WIDTH)),
        _const_spec((D_MODEL, D_MODEL)),
    ]
    out_specs = [
        seq_spec,
        pl.BlockSpec((nb, N_HEADS, HEAD_DIM, HEAD_DIM), lambda b, i: (b, 0, 0, 0)),
        pl.BlockSpec((nb, 1, RG_WIDTH), lambda b, i: (b, 0, 0)),
        pl.BlockSpec((nb, CONV_WIDTH - 1, RG_WIDTH), lambda b, i: (b, 0, 0)),
    ]
    out_shape = [
        jax.ShapeDtypeStruct((bsz, seq, D_MODEL), F32),
        jax.ShapeDtypeStruct((bsz, N_HEADS, HEAD_DIM, HEAD_DIM), F32),
        jax.ShapeDtypeStruct((bsz, 1, RG_WIDTH), F32),
        jax.ShapeDtypeStruct((bsz, CONV_WIDTH - 1, RG_WIDTH), F32),
    ]
    scratch = [
        pltpu.VMEM((rows, D_MODEL), BF16),
        pltpu.VMEM((rows, IN_WIDTH), F32),
        pltpu.VMEM((nb, SUBLANES + t, RG_WIDTH), F32),
        pltpu.VMEM((rows, 2 * RG_WIDTH), F32),
        pltpu.VMEM((rows, RG_WIDTH), F32),
        pltpu.VMEM((rows, D_MODEL), BF16),
        pltpu.VMEM((nb, N_HEADS, HEAD_DIM, HEAD_DIM), F32),
        pltpu.VMEM((nb, 1, RG_WIDTH), F32),
        pltpu.VMEM((D_MODEL, IN_WIDTH), BF16),
        pltpu.VMEM((RG_WIDTH, 2 * RG_WIDTH), BF16),
        pltpu.VMEM((D_MODEL, D_MODEL), BF16),
    ]
    return pl.pallas_call(
        kern,
        grid=grid,
        in_specs=in_specs,
        out_specs=out_specs,
        out_shape=out_shape,
        scratch_shapes=scratch,
        compiler_params=pltpu.CompilerParams(
            dimension_semantics=("arbitrary", "arbitrary"), vmem_limit_bytes=VMEM_LIMIT),
        name="mixer",
    )(x, mod, s0, h0, cb0, lbl, w_in, gain, conv_w, conv_b, w_rg, b_rg, lam, w_out)


def _mlp_kernel(x1_ref, mod_ref, wup_ref, wdn_ref, fgain_ref, y_ref, hn_s, *, nb, t):
    for n in range(nb):
        m = mod_ref[n]
        sh2 = m[:, 3 * D_MODEL:4 * D_MODEL]
        sc2 = m[:, 4 * D_MODEL:5 * D_MODEL]
        hn_s[n * t:(n + 1) * t, :] = (_rms(x1_ref[n]) * (1.0 + sc2) + sh2).astype(BF16)
    hn = hn_s[...]
    acc = None
    for j in range(D_FF // D_MODEL):
        up = _dot(hn, wup_ref[:, j * D_MODEL:(j + 1) * D_MODEL])
        act = jnp.square(jnp.maximum(up, 0.0)).astype(BF16)
        dn = _dot(act, wdn_ref[j * D_MODEL:(j + 1) * D_MODEL, :])
        acc = dn if acc is None else acc + dn
    for n in range(nb):
        g2 = mod_ref[n][:, 5 * D_MODEL:6 * D_MODEL]
        x2 = x1_ref[n] + g2 * acc[n * t:(n + 1) * t, :]
        y_ref[n] = _rms(x2) * fgain_ref[...]


def _mlp(x1, mod, w_up, w_down, fgain, *, nb, t):
    bsz, seq, _ = x1.shape
    grid = (bsz // nb, seq // t)
    kern = functools.partial(_mlp_kernel, nb=nb, t=t)
    seq_spec = pl.BlockSpec((nb, t, D_MODEL), lambda b, i: (b, i, 0))
    return pl.pallas_call(
        kern,
        grid=grid,
        in_specs=[
            seq_spec,
            pl.BlockSpec((nb, 1, N_MOD * D_MODEL), lambda b, i: (b, 0, 0)),
            _const_spec((D_MODEL, D_FF)),
            _const_spec((D_FF, D_MODEL)),
            _const_spec((1, D_MODEL)),
        ],
        out_specs=seq_spec,
        out_shape=jax.ShapeDtypeStruct((bsz, seq, D_MODEL), F32),
        scratch_shapes=[pltpu.VMEM((nb * t, D_MODEL), BF16)],
        compiler_params=pltpu.CompilerParams(
            dimension_semantics=("arbitrary", "arbitrary"), vmem_limit_bytes=VMEM_LIMIT),
        name="mlp",
    )(x1, mod, w_up, w_down, fgain)


def _block_diag(w):
    n, c, d = w.shape
    eye = jnp.eye(n, dtype=w.dtype)
    return (eye[:, None, :, None] * w[:, :, None, :]).reshape(n * c, n * d)


def kernel(x_prompt, x_sample, c_prompt, c_sample, state_hgrn, state_rglru, cache_conv,
           hg_lb_logits, w_ada, b_ada, w_in, hg_norm_gain, conv_w, conv_b,
           rg_wa, rg_ba, rg_wx, rg_bx, rg_lambda, w_out, w_up, w_down, final_gain):
    bp = x_prompt.shape[0]
    bs = x_sample.shape[0]

    w_in_b = w_in[0].astype(BF16)
    w_out_b = w_out[0].astype(BF16)
    w_up_b = w_up[0].astype(BF16)
    w_down_b = w_down[0].astype(BF16)
    w_rg = jnp.concatenate([_block_diag(rg_wa[0]), _block_diag(rg_wx[0])], axis=1).astype(BF16)
    b_rg = jnp.concatenate([rg_ba[0], rg_bx[0]])[None, :]
    gain = hg_norm_gain[0][None, :]
    cbias = conv_b[0][None, :]
    lam = rg_lambda[0][None, :]
    fgain = final_gain[None, :]

    c_all = jnp.concatenate([c_prompt, c_sample], axis=0)
    mod = _ada(c_all, w_ada[0], b_ada[0][None, :])
    mod_p = mod[:bp][:, None, :]
    mod_s = mod[bp:][:, None, :]

    def trunk(x, mod_x, s0, h0, cb0, nb, t):
        x1, s_new, h_new, cb_new = _mixer(
            x, mod_x, s0, h0[:, None, :], cb0, hg_lb_logits, w_in_b, gain, conv_w[0], cbias,
            w_rg, b_rg, lam, w_out_b, nb=nb, t=t)
        y = _mlp(x1, mod_x, w_up_b, w_down_b, fgain, nb=nb, t=t)
        return y, s_new[None], h_new[:, 0, :][None], cb_new[None]

    s0p = jnp.zeros((bp, N_HEADS, HEAD_DIM, HEAD_DIM), state_hgrn.dtype)
    h0p = jnp.zeros((bp, RG_WIDTH), state_rglru.dtype)
    cbp = jnp.zeros((bp, CONV_WIDTH - 1, RG_WIDTH), cache_conv.dtype)
    y_p, s_p, h_p, cb_p = trunk(x_prompt, mod_p, s0p, h0p, cbp, 1, 256)
    y_s, s_s, h_s, cb_s = trunk(x_sample, mod_s, state_hgrn[0], state_rglru[0], cache_conv[0],
                                1, x_sample.shape[1])
    return (y_p, y_s, s_p, h_p, cb_p, s_s, h_s, cb_s)
```

```python
import functools

import jax
import jax.numpy as jnp
from jax import lax
from jax.experimental import pallas as pl
from jax.experimental.pallas import tpu as pltpu

F32 = jnp.float32
BF16 = jnp.bfloat16

D_MODEL = 1024
HG_WIDTH = 512
RG_WIDTH = 512
HEAD_DIM = 128
N_HEADS = 4
N_MOD = 6
D_FF = 4096
IN_WIDTH = 4 * HG_WIDTH + 2 * RG_WIDTH
CONV_WIDTH = 4
RG_C = 8.0
EPS = 1e-6

SUBLANES = 8
CHUNK = 64
N_GROUPS = CHUNK // SUBLANES
VMEM_LIMIT = 52 * 1024 * 1024
MAX_SEQ_ROWS = 512
MIN_STEP_ROWS = 512


def _block_rows(bsz, seq):
    t = min(seq, MAX_SEQ_ROWS)
    assert seq % t == 0 and t % CHUNK == 0
    nb = 1
    if t < MAX_SEQ_ROWS:
        nb = min(bsz, MIN_STEP_ROWS // t)
        while bsz % nb:
            nb -= 1
    return nb, t


def _sigmoid(x):
    return 0.5 * jnp.tanh(0.5 * x) + 0.5


def _silu(x):
    h = 0.5 * x
    return h + h * jnp.tanh(h)


def _gelu_tanh(x):
    c = 0.7978845608028654
    return 0.5 * x * (1.0 + jnp.tanh(c * (x + 0.044715 * (x * x * x))))


def _rms(x):
    return x * lax.rsqrt(jnp.mean(x * x, axis=-1, keepdims=True) + EPS)


def _dot(a, b):
    return jnp.dot(a, b, preferred_element_type=F32)


def _dot_nt(a, b):
    return lax.dot_general(a, b, (((1,), (1,)), ((), ())), preferred_element_type=F32)


def _ada_kernel(c_ref, w_ref, b_ref, o_ref):
    c = c_ref[...]
    o_ref[...] = _dot(_silu(c).astype(BF16), w_ref[...].astype(BF16)) + b_ref[...]


def _ada(c_all, w_ada, b_ada):
    rows = c_all.shape[0]
    tn = 1536
    return pl.pallas_call(
        _ada_kernel,
        grid=(N_MOD * D_MODEL // tn,),
        in_specs=[
            pl.BlockSpec((rows, D_MODEL), lambda j: (0, 0)),
            pl.BlockSpec((D_MODEL, tn), lambda j: (0, j)),
            pl.BlockSpec((1, tn), lambda j: (0, j)),
        ],
        out_specs=pl.BlockSpec((rows, tn), lambda j: (0, j)),
        out_shape=jax.ShapeDtypeStruct((rows, N_MOD * D_MODEL), F32),
        compiler_params=pltpu.CompilerParams(
            dimension_semantics=("arbitrary",), vmem_limit_bytes=VMEM_LIMIT),
        name="ada_mod",
    )(c_all, w_ada, b_ada)


def _group_scan_products(f3, r8):
    pf = f3
    for sh in (1, 2, 4):
        pf = jnp.where(r8 >= sh, pf * pltpu.roll(pf, sh, 1), pf)
    pb = jnp.where(r8 < SUBLANES - 1, pltpu.roll(f3, SUBLANES - 1, 1), 1.0)
    for sh in (1, 2, 4):
        pb = jnp.where(r8 < SUBLANES - sh, pb * pltpu.roll(pb, SUBLANES - sh, 1), pb)
    return pf, pb


def _block_products(pf8, pb8):
    out = {SUBLANES: (pf8, pb8)}
    curf, curb = pf8, pb8
    h = SUBLANES
    while h < CHUNK:
        g = h // SUBLANES
        nf, nb = [], []
        for grp in range(N_GROUPS):
            base = (grp // (2 * g)) * 2 * g
            if grp % (2 * g) >= g:
                total_lower = curf[base + g - 1][SUBLANES - 1:SUBLANES, :]
                nf.append(curf[grp] * total_lower)
                nb.append(curb[grp])
            else:
                total_upper = curf[base + 2 * g - 1][SUBLANES - 1:SUBLANES, :]
                nf.append(curf[grp])
                nb.append(curb[grp] * total_upper)
        curf, curb = nf, nb
        h *= 2
        out[h] = (curf, curb)
    return out


def _hgrn2_chunk(q, fg, k, v, st, r8, band_diff, level_masks):
    q3 = q.reshape(N_GROUPS, SUBLANES, HEAD_DIM)
    f3 = fg.reshape(N_GROUPS, SUBLANES, HEAD_DIM)
    k3 = k.reshape(N_GROUPS, SUBLANES, HEAD_DIM)

    s = jnp.sum(q3 * k3, axis=-1, keepdims=True).reshape(CHUNK, 1)
    scores = jnp.where(band_diff == 0, s, 0.0)
    kd = k3
    for delta in range(1, SUBLANES):
        kd = f3 * pltpu.roll(kd, 1, 1)
        s = jnp.sum(q3 * kd, axis=-1, keepdims=True).reshape(CHUNK, 1)
        scores = jnp.where(band_diff == delta, s, scores)

    pf8, pb8 = _group_scan_products(f3, r8)
    prods = _block_products([pf8[g] for g in range(N_GROUPS)],
                            [pb8[g] for g in range(N_GROUPS)])
    qg = [q3[g] for g in range(N_GROUPS)]
    kg = [k3[g] for g in range(N_GROUPS)]

    def scaled(tiles, facs):
        return jnp.concatenate([t * p for t, p in zip(tiles, facs)], axis=0).astype(BF16)

    h = SUBLANES
    while h < CHUNK:
        a = _dot_nt(scaled(qg, prods[h][0]), scaled(kg, prods[h][1]))
        scores = jnp.where(level_masks[h], a, scores)
        h *= 2
    o = _dot(scores.astype(BF16), v.astype(BF16))

    o = o + _dot_nt(scaled(qg, prods[CHUNK][0]), st.astype(BF16))
    total = prods[CHUNK][0][N_GROUPS - 1][SUBLANES - 1:SUBLANES, :]
    st_new = st * total + _dot(v.T.astype(BF16), scaled(kg, prods[CHUNK][1]))
    return o, st_new


def _mixer_kernel(x_ref, mod_ref, s0_ref, h0_ref, cb0_ref, lbl_ref, win_ref, gain_ref,
                  cw_ref, cbias_ref, wrg_ref, brg_ref, lam_ref, wout_ref,
                  x1_ref, s_ref, h_ref, cbo_ref,
                  hn_s, proj_s, xr_s, gate_s, xc_s, omix_s, st_s, hc_s,
                  win_s, wrg_s, wout_s, *, nb, t):
    i = pl.program_id(1)
    last = pl.num_programs(1) - 1
    tail = CONV_WIDTH - 1
    pad = SUBLANES

    @pl.when((pl.program_id(0) == 0) & (i == 0))
    def _():
        win_s[...] = win_ref[...]
        wrg_s[...] = wrg_ref[...]
        wout_s[...] = wout_ref[...]

    @pl.when(i == 0)
    def _():
        for n in range(nb):
            for h in range(N_HEADS):
                st_s[n, h] = s0_ref[n, h].T
            hc_s[n] = h0_ref[n]
            xr_s[n, 0:pad, :] = jnp.zeros((pad, RG_WIDTH), F32)
            xr_s[n, pad - tail:pad, :] = cb0_ref[n]

    for n in range(nb):
        m = mod_ref[n]
        sh1 = m[:, 0:D_MODEL]
        sc1 = m[:, D_MODEL:2 * D_MODEL]
        hn = _rms(x_ref[n]) * (1.0 + sc1) + sh1
        hn_s[n * t:(n + 1) * t, :] = hn.astype(BF16)
    proj_s[...] = _dot(hn_s[...], win_s[...])

    cw = cw_ref[...]
    for n in range(nb):
        xr_s[n, pad:pad + t, :] = proj_s[n * t:(n + 1) * t, 4 * HG_WIDTH:4 * HG_WIDTH + RG_WIDTH]
        xc = cbias_ref[...] + xr_s[n, pad - tail:pad - tail + t, :] * cw[0:1, :]
        for j in range(1, CONV_WIDTH):
            xc = xc + xr_s[n, pad - tail + j:pad - tail + j + t, :] * cw[j:j + 1, :]
        xc_s[n * t:(n + 1) * t, :] = xc
    gate_s[...] = _dot(xc_s[...].astype(BF16), wrg_s[...]) + brg_ref[...]

    lbl = lbl_ref[...]
    lmax = jnp.maximum(lbl[0:1, :], lbl[1:2, :])
    e0 = jnp.exp(lbl[0:1, :] - lmax)
    e1 = jnp.exp(lbl[1:2, :] - lmax)
    lb = e0 / (e0 + e1)
    fg_mid = 0.5 * (1.0 + lb)
    fg_half = 0.5 * (1.0 - lb)
    nlam = -lam_ref[...]
    ez = jnp.exp(-jnp.abs(nlam))
    one_p = 1.0 + ez
    log1p_ez = jnp.where(one_p == 1.0, ez, jnp.log(one_p) * (ez / (one_p - 1.0)))
    softplus_nlam = jnp.maximum(nlam, 0.0) + log1p_ez
    quarter_rate = (-0.25 * RG_C) * softplus_nlam
    gain = gain_ref[...]

    r8 = lax.broadcasted_iota(jnp.int32, (N_GROUPS, SUBLANES, HEAD_DIM), 1)
    r8w = lax.broadcasted_iota(jnp.int32, (N_GROUPS, SUBLANES, RG_WIDTH), 1)
    ti = lax.broadcasted_iota(jnp.int32, (CHUNK, CHUNK), 0)
    si = lax.broadcasted_iota(jnp.int32, (CHUNK, CHUNK), 1)
    grp = SUBLANES.bit_length() - 1
    band_diff = jnp.where((ti >> grp) == (si >> grp), ti - si, -1)
    level_masks = {}
    h = SUBLANES
    while h < CHUNK:
        shift = h.bit_length() - 1
        level_masks[h] = (((ti ^ si) >> shift) == 1) & (ti > si)
        h *= 2

    for n in range(nb):
        for c in range(t // CHUNK):
            rows = pl.ds(n * t + c * CHUNK, CHUNK)
            for hd in range(N_HEADS):
                c0 = hd * HEAD_DIM
                cols = slice(c0, c0 + HEAD_DIM)
                q = _silu(proj_s[rows, c0:c0 + HEAD_DIM])
                kt = fg_half[:, cols] * jnp.tanh(
                    0.5 * proj_s[rows, HG_WIDTH + c0:HG_WIDTH + c0 + HEAD_DIM])
                fg = fg_mid[:, cols] + kt
                k = fg_half[:, cols] - kt
                v = proj_s[rows, 2 * HG_WIDTH + c0:2 * HG_WIDTH + c0 + HEAD_DIM]
                og = proj_s[rows, 3 * HG_WIDTH + c0:3 * HG_WIDTH + c0 + HEAD_DIM]
                o, st_new = _hgrn2_chunk(q, fg, k, v, st_s[n, hd], r8, band_diff, level_masks)
                st_s[n, hd] = st_new
                o = _rms(o) * gain[:, cols] * _silu(og)
                omix_s[rows, c0:c0 + HEAD_DIM] = o.astype(BF16)
            xc = xc_s[rows, :]
            half_z = (1.0 + jnp.tanh(0.5 * gate_s[rows, 0:RG_WIDTH])) * quarter_rate
            tp = jnp.tanh(half_z)
            w = 1.0 / (1.0 - tp)
            a = (1.0 + tp) * w
            ig2 = 1.0 + jnp.tanh(0.5 * gate_s[rows, RG_WIDTH:2 * RG_WIDTH])
            u = (jnp.sqrt(-tp) * w) * (ig2 * xc)
            ca = a.reshape(N_GROUPS, SUBLANES, RG_WIDTH)
            cu = u.reshape(N_GROUPS, SUBLANES, RG_WIDTH)
            for sh in (1, 2, 4):
                keep = r8w >= sh
                cu = jnp.where(keep, ca * pltpu.roll(cu, sh, 1) + cu, cu)
                ca = jnp.where(keep, ca * pltpu.roll(ca, sh, 1), ca)
            hprev = hc_s[n]
            hs = []
            for g in range(N_GROUPS):
                hg = ca[g] * hprev + cu[g]
                hs.append(hg)
                hprev = hg[SUBLANES - 1:SUBLANES, :]
            hc_s[n] = hprev
            hseq = jnp.concatenate(hs, axis=0)
            gr = proj_s[rows, 4 * HG_WIDTH + RG_WIDTH:IN_WIDTH]
            omix_s[rows, HG_WIDTH:HG_WIDTH + RG_WIDTH] = (hseq * _gelu_tanh(gr)).astype(BF16)

    y = _dot(omix_s[...], wout_s[...])
    for n in range(nb):
        g1 = mod_ref[n][:, 2 * D_MODEL:3 * D_MODEL]
        x1_ref[n] = x_ref[n] + g1 * y[n * t:(n + 1) * t, :]

    @pl.when(i == last)
    def _():
        for n in range(nb):
            for h in range(N_HEADS):
                s_ref[n, h] = st_s[n, h].T
            h_ref[n] = hc_s[n]
            cbo_ref[n] = xr_s[n, pad + t - tail:pad + t, :]

    for n in range(nb):
        xr_s[n, pad - tail:pad, :] = xr_s[n, pad + t - tail:pad + t, :]


def _const_spec(shape):
    zeros = (0,) * len(shape)
    return pl.BlockSpec(shape, lambda b, i: zeros, pipeline_mode=pl.Buffered(1))


def _mixer(x, mod, s0, h0, cb0, lbl, w_in, gain, conv_w, conv_b, w_rg, b_rg, lam, w_out, *, nb, t):
    bsz, seq, _ = x.shape
    grid = (bsz // nb, seq // t)
    rows = nb * t
    kern = functools.partial(_mixer_kernel, nb=nb, t=t)
    seq_spec = pl.BlockSpec((nb, t, D_MODEL), lambda b, i: (b, i, 0))
    in_specs = [
        seq_spec,
        pl.BlockSpec((nb, 1, N_MOD * D_MODEL), lambda b, i: (b, 0, 0)),
        pl.BlockSpec((nb, N_HEADS, HEAD_DIM, HEAD_DIM), lambda b, i: (b, 0, 0, 0)),
        pl.BlockSpec((nb, 1, RG_WIDTH), lambda b, i: (b, 0, 0)),
        pl.BlockSpec((nb, CONV_WIDTH - 1, RG_WIDTH), lambda b, i: (b, 0, 0)),
        _const_spec((2, HG_WIDTH)),
        _const_spec((D_MODEL, IN_WIDTH)),
        _const_spec((1, HG_WIDTH)),
        _const_spec((CONV_WIDTH, RG_WIDTH)),
        _const_spec((1, RG_WIDTH)),
        _const_spec((RG_WIDTH, 2 * RG_WIDTH)),
        _const_spec((1, 2 * RG_WIDTH)),
        _const_spec((1, RG_WIDTH)),
        _const_spec((D_MODEL, D_MODEL)),
    ]
    out_specs = [
        seq_spec,
        pl.BlockSpec((nb, N_HEADS, HEAD_DIM, HEAD_DIM), lambda b, i: (b, 0, 0, 0)),
        pl.BlockSpec((nb, 1, RG_WIDTH), lambda b, i: (b, 0, 0)),
        pl.BlockSpec((nb, CONV_WIDTH - 1, RG_WIDTH), lambda b, i: (b, 0, 0)),
    ]
    out_shape = [
        jax.ShapeDtypeStruct((bsz, seq, D_MODEL), F32),
        jax.ShapeDtypeStruct((bsz, N_HEADS, HEAD_DIM, HEAD_DIM), F32),
        jax.ShapeDtypeStruct((bsz, 1, RG_WIDTH), F32),
        jax.ShapeDtypeStruct((bsz, CONV_WIDTH - 1, RG_WIDTH), F32),
    ]
    scratch = [
        pltpu.VMEM((rows, D_MODEL), BF16),
        pltpu.VMEM((rows, IN_WIDTH), F32),
        pltpu.VMEM((nb, SUBLANES + t, RG_WIDTH), F32),
        pltpu.VMEM((rows, 2 * RG_WIDTH), F32),
        pltpu.VMEM((rows, RG_WIDTH), F32),
        pltpu.VMEM((rows, D_MODEL), BF16),
        pltpu.VMEM((nb, N_HEADS, HEAD_DIM, HEAD_DIM), F32),
        pltpu.VMEM((nb, 1, RG_WIDTH), F32),
        pltpu.VMEM((D_MODEL, IN_WIDTH), BF16),
        pltpu.VMEM((RG_WIDTH, 2 * RG_WIDTH), BF16),
        pltpu.VMEM((D_MODEL, D_MODEL), BF16),
    ]
    return pl.pallas_call(
        kern,
        grid=grid,
        in_specs=in_specs,
        out_specs=out_specs,
        out_shape=out_shape,
        scratch_shapes=scratch,
        compiler_params=pltpu.CompilerParams(
            dimension_semantics=("arbitrary", "arbitrary"), vmem_limit_bytes=VMEM_LIMIT),
        name="mixer",
    )(x, mod, s0, h0, cb0, lbl, w_in, gain, conv_w, conv_b, w_rg, b_rg, lam, w_out)


def _mlp_kernel(x1_ref, mod_ref, wup_ref, wdn_ref, fgain_ref, y_ref, hn_s, *, nb, t):
    for n in range(nb):
        m = mod_ref[n]
        sh2 = m[:, 3 * D_MODEL:4 * D_MODEL]
        sc2 = m[:, 4 * D_MODEL:5 * D_MODEL]
        hn_s[n * t:(n + 1) * t, :] = (_rms(x1_ref[n]) * (1.0 + sc2) + sh2).astype(BF16)
    hn = hn_s[...]
    acc = None
    for j in range(D_FF // D_MODEL):
        up = _dot(hn, wup_ref[:, j * D_MODEL:(j + 1) * D_MODEL])
        act = jnp.square(jnp.maximum(up, 0.0)).astype(BF16)
        dn = _dot(act, wdn_ref[j * D_MODEL:(j + 1) * D_MODEL, :])
        acc = dn if acc is None else acc + dn
    for n in range(nb):
        g2 = mod_ref[n][:, 5 * D_MODEL:6 * D_MODEL]
        x2 = x1_ref[n] + g2 * acc[n * t:(n + 1) * t, :]
        y_ref[n] = _rms(x2) * fgain_ref[...]


def _mlp(x1, mod, w_up, w_down, fgain, *, nb, t):
    bsz, seq, _ = x1.shape
    grid = (bsz // nb, seq // t)
    kern = functools.partial(_mlp_kernel, nb=nb, t=t)
    seq_spec = pl.BlockSpec((nb, t, D_MODEL), lambda b, i: (b, i, 0))
    return pl.pallas_call(
        kern,
        grid=grid,
        in_specs=[
            seq_spec,
            pl.BlockSpec((nb, 1, N_MOD * D_MODEL), lambda b, i: (b, 0, 0)),
            _const_spec((D_MODEL, D_FF)),
            _const_spec((D_FF, D_MODEL)),
            _const_spec((1, D_MODEL)),
        ],
        out_specs=seq_spec,
        out_shape=jax.ShapeDtypeStruct((bsz, seq, D_MODEL), F32),
        scratch_shapes=[pltpu.VMEM((nb * t, D_MODEL), BF16)],
        compiler_params=pltpu.CompilerParams(
            dimension_semantics=("arbitrary", "arbitrary"), vmem_limit_bytes=VMEM_LIMIT),
        name="mlp",
    )(x1, mod, w_up, w_down, fgain)


def _block_diag(w):
    n, c, d = w.shape
    eye = jnp.eye(n, dtype=w.dtype)
    return (eye[:, None, :, None] * w[:, :, None, :]).reshape(n * c, n * d)


def kernel(x_prompt, x_sample, c_prompt, c_sample, state_hgrn, state_rglru, cache_conv,
           hg_lb_logits, w_ada, b_ada, w_in, hg_norm_gain, conv_w, conv_b,
           rg_wa, rg_ba, rg_wx, rg_bx, rg_lambda, w_out, w_up, w_down, final_gain):
    bp = x_prompt.shape[0]
    bs = x_sample.shape[0]

    w_in_b = w_in[0].astype(BF16)
    w_out_b = w_out[0].astype(BF16)
    w_up_b = w_up[0].astype(BF16)
    w_down_b = w_down[0].astype(BF16)
    w_rg = jnp.concatenate([_block_diag(rg_wa[0]), _block_diag(rg_wx[0])], axis=1).astype(BF16)
    b_rg = jnp.concatenate([rg_ba[0], rg_bx[0]])[None, :]
    gain = hg_norm_gain[0][None, :]
    cbias = conv_b[0][None, :]
    lam = rg_lambda[0][None, :]
    fgain = final_gain[None, :]

    c_all = jnp.concatenate([c_prompt, c_sample], axis=0)
    mod = _ada(c_all, w_ada[0], b_ada[0][None, :])
    mod_p = mod[:bp][:, None, :]
    mod_s = mod[bp:][:, None, :]

    def trunk(x, mod_x, s0, h0, cb0):
        nb, t = _block_rows(x.shape[0], x.shape[1])
        x1, s_new, h_new, cb_new = _mixer(
            x, mod_x, s0, h0[:, None, :], cb0, hg_lb_logits, w_in_b, gain, conv_w[0], cbias,
            w_rg, b_rg, lam, w_out_b, nb=nb, t=t)
        y = _mlp(x1, mod_x, w_up_b, w_down_b, fgain, nb=nb, t=t)
        return y, s_new[None], h_new[:, 0, :][None], cb_new[None]

    s0p = jnp.zeros((bp, N_HEADS, HEAD_DIM, HEAD_DIM), state_hgrn.dtype)
    h0p = jnp.zeros((bp, RG_WIDTH), state_rglru.dtype)
    cbp = jnp.zeros((bp, CONV_WIDTH - 1, RG_WIDTH), cache_conv.dtype)
    y_p, s_p, h_p, cb_p = trunk(x_prompt, mod_p, s0p, h0p, cbp)
    y_s, s_s, h_s, cb_s = trunk(x_sample, mod_s, state_hgrn[0], state_rglru[0], cache_conv[0])
    return (y_p, y_s, s_p, h_p, cb_p, s_s, h_s, cb_s)
```

```python
import functools

import jax
import jax.numpy as jnp
from jax import lax
from jax.experimental import pallas as pl
from jax.experimental.pallas import tpu as pltpu

F32 = jnp.float32
BF16 = jnp.bfloat16

D_MODEL = 1024
HG_WIDTH = 512
RG_WIDTH = 512
HEAD_DIM = 128
N_HEADS = 4
N_MOD = 6
D_FF = 4096
IN_WIDTH = 4 * HG_WIDTH + 2 * RG_WIDTH
HEAD_COLS = 4 * HEAD_DIM
Q_OFF, F_OFF, V_OFF, G_OFF = 0, HEAD_DIM, 2 * HEAD_DIM, 3 * HEAD_DIM
XR_COL = N_HEADS * HEAD_COLS
GR_COL = XR_COL + RG_WIDTH
CONV_WIDTH = 4
RG_C = 8.0
EPS = 1e-6

SUBLANES = 8
CHUNK = 64
N_GROUPS = CHUNK // SUBLANES
VMEM_LIMIT = 52 * 1024 * 1024
MAX_SEQ_ROWS = 512
MIN_STEP_ROWS = 512


def _block_rows(bsz, seq):
    t = min(seq, MAX_SEQ_ROWS)
    assert seq % t == 0 and t % CHUNK == 0
    nb = 1
    if t < MAX_SEQ_ROWS:
        nb = min(bsz, MIN_STEP_ROWS // t)
        while bsz % nb:
            nb -= 1
    return nb, t


def _sigmoid(x):
    return 0.5 * jnp.tanh(0.5 * x) + 0.5


def _silu(x):
    h = 0.5 * x
    return h + h * jnp.tanh(h)


def _gelu_tanh(x):
    c = 0.7978845608028654
    return 0.5 * x * (1.0 + jnp.tanh(c * (x + 0.044715 * (x * x * x))))


def _rms(x):
    return x * lax.rsqrt(jnp.mean(x * x, axis=-1, keepdims=True) + EPS)


def _dot(a, b):
    return jnp.dot(a, b, preferred_element_type=F32)


def _dot_nt(a, b):
    return lax.dot_general(a, b, (((1,), (1,)), ((), ())), preferred_element_type=F32)


def _ada_kernel(c_ref, w_ref, b_ref, o_ref):
    c = c_ref[...]
    o_ref[...] = _dot(_silu(c).astype(BF16), w_ref[...].astype(BF16)) + b_ref[...]


def _ada(c_all, w_ada, b_ada):
    rows = c_all.shape[0]
    tn = 1536
    return pl.pallas_call(
        _ada_kernel,
        grid=(N_MOD * D_MODEL // tn,),
        in_specs=[
            pl.BlockSpec((rows, D_MODEL), lambda j: (0, 0)),
            pl.BlockSpec((D_MODEL, tn), lambda j: (0, j)),
            pl.BlockSpec((1, tn), lambda j: (0, j)),
        ],
        out_specs=pl.BlockSpec((rows, tn), lambda j: (0, j)),
        out_shape=jax.ShapeDtypeStruct((rows, N_MOD * D_MODEL), F32),
        compiler_params=pltpu.CompilerParams(
            dimension_semantics=("arbitrary",), vmem_limit_bytes=VMEM_LIMIT),
        name="ada_mod",
    )(c_all, w_ada, b_ada)


def _group_scan_products(f3, r8):
    pf = f3
    for sh in (1, 2, 4):
        pf = jnp.where(r8 >= sh, pf * pltpu.roll(pf, sh, 1), pf)
    pb = jnp.where(r8 < SUBLANES - 1, pltpu.roll(f3, SUBLANES - 1, 1), 1.0)
    for sh in (1, 2, 4):
        pb = jnp.where(r8 < SUBLANES - sh, pb * pltpu.roll(pb, SUBLANES - sh, 1), pb)
    return pf, pb


def _block_products(pf8, pb8):
    out = {SUBLANES: (pf8, pb8)}
    curf, curb = pf8, pb8
    h = SUBLANES
    while h < CHUNK:
        g = h // SUBLANES
        nf, nb = [], []
        for grp in range(N_GROUPS):
            base = (grp // (2 * g)) * 2 * g
            if grp % (2 * g) >= g:
                total_lower = curf[base + g - 1][SUBLANES - 1:SUBLANES, :]
                nf.append(curf[grp] * total_lower)
                nb.append(curb[grp])
            else:
                total_upper = curf[base + 2 * g - 1][SUBLANES - 1:SUBLANES, :]
                nf.append(curf[grp])
                nb.append(curb[grp] * total_upper)
        curf, curb = nf, nb
        h *= 2
        out[h] = (curf, curb)
    return out


def _hgrn2_chunk(q, fg, k, v, st, r8, pair_ref):
    q3 = q.reshape(N_GROUPS, SUBLANES, HEAD_DIM)
    f3 = fg.reshape(N_GROUPS, SUBLANES, HEAD_DIM)
    k3 = k.reshape(N_GROUPS, SUBLANES, HEAD_DIM)

    s = jnp.sum(q3 * k3, axis=-1, keepdims=True).reshape(CHUNK, 1)
    scores = jnp.where(pair_ref[...] == 0, s, 0.0)
    kd = k3
    for delta in range(1, SUBLANES):
        kd = f3 * pltpu.roll(kd, 1, 1)
        s = jnp.sum(q3 * kd, axis=-1, keepdims=True).reshape(CHUNK, 1)
        scores = jnp.where(pair_ref[...] == delta, s, scores)

    pf8, pb8 = _group_scan_products(f3, r8)
    prods = _block_products([pf8[g] for g in range(N_GROUPS)],
                            [pb8[g] for g in range(N_GROUPS)])
    qg = [q3[g] for g in range(N_GROUPS)]
    kg = [k3[g] for g in range(N_GROUPS)]

    def scaled(tiles, facs):
        return jnp.concatenate([t * p for t, p in zip(tiles, facs)], axis=0).astype(BF16)

    h = SUBLANES
    level = SUBLANES
    while h < CHUNK:
        a = _dot_nt(scaled(qg, prods[h][0]), scaled(kg, prods[h][1]))
        scores = jnp.where(pair_ref[...] == level, a, scores)
        h *= 2
        level += 1
    o = _dot(scores.astype(BF16), v.astype(BF16))

    o = o + _dot_nt(scaled(qg, prods[CHUNK][0]), st.astype(BF16))
    total = prods[CHUNK][0][N_GROUPS - 1][SUBLANES - 1:SUBLANES, :]
    st_new = st * total + _dot(v.T.astype(BF16), scaled(kg, prods[CHUNK][1]))
    return o, st_new


def _mixer_kernel(x_ref, mod_ref, s0_ref, h0_ref, cb0_ref, lbl_ref, win_ref, gain_ref,
                  cw_ref, cbias_ref, wrg_ref, brg_ref, lam_ref, wout_ref,
                  x1_ref, s_ref, h_ref, cbo_ref,
                  hn_s, proj_s, xr_s, gate_s, xc_s, omix_s, st_s, hc_s,
                  win_s, wrg_s, wout_s, pair_s, *, nb, t):
    i = pl.program_id(1)
    last = pl.num_programs(1) - 1
    tail = CONV_WIDTH - 1
    pad = SUBLANES

    @pl.when((pl.program_id(0) == 0) & (i == 0))
    def _():
        win_s[...] = win_ref[...]
        wrg_s[...] = wrg_ref[...]
        wout_s[...] = wout_ref[...]

    @pl.when(i == 0)
    def _():
        for n in range(nb):
            for h in range(N_HEADS):
                st_s[n, h] = s0_ref[n, h].T
            hc_s[n] = h0_ref[n]
            xr_s[n, 0:pad, :] = jnp.zeros((pad, RG_WIDTH), F32)
            xr_s[n, pad - tail:pad, :] = cb0_ref[n]

    for n in range(nb):
        m = mod_ref[n]
        sh1 = m[:, 0:D_MODEL]
        sc1 = m[:, D_MODEL:2 * D_MODEL]
        hn = _rms(x_ref[n]) * (1.0 + sc1) + sh1
        hn_s[n * t:(n + 1) * t, :] = hn.astype(BF16)

    def project(c0, c1):
        proj_s[:, c0:c1] = _dot(hn_s[...], win_s[:, c0:c1])

    project(XR_COL, GR_COL)

    cw = cw_ref[...]
    for n in range(nb):
        xr_s[n, pad:pad + t, :] = proj_s[n * t:(n + 1) * t, XR_COL:GR_COL]
        xc = cbias_ref[...] + xr_s[n, pad - tail:pad - tail + t, :] * cw[0:1, :]
        for j in range(1, CONV_WIDTH):
            xc = xc + xr_s[n, pad - tail + j:pad - tail + j + t, :] * cw[j:j + 1, :]
        xc_s[n * t:(n + 1) * t, :] = xc
    project(0, 2 * HEAD_COLS)
    gate_s[...] = _dot(xc_s[...].astype(BF16), wrg_s[...]) + brg_ref[...]
    project(2 * HEAD_COLS, XR_COL)
    project(GR_COL, IN_WIDTH)

    lbl = lbl_ref[...]
    lmax = jnp.maximum(lbl[0:1, :], lbl[1:2, :])
    e0 = jnp.exp(lbl[0:1, :] - lmax)
    e1 = jnp.exp(lbl[1:2, :] - lmax)
    lb = e0 / (e0 + e1)
    fg_mid = 0.5 * (1.0 + lb)
    fg_half = 0.5 * (1.0 - lb)
    nlam = -lam_ref[...]
    ez = jnp.exp(-jnp.abs(nlam))
    one_p = 1.0 + ez
    log1p_ez = jnp.where(one_p == 1.0, ez, jnp.log(one_p) * (ez / (one_p - 1.0)))
    softplus_nlam = jnp.maximum(nlam, 0.0) + log1p_ez
    quarter_rate = (-0.25 * RG_C) * softplus_nlam
    gain = gain_ref[...]

    r8 = lax.broadcasted_iota(jnp.int32, (N_GROUPS, SUBLANES, HEAD_DIM), 1)
    r8w = lax.broadcasted_iota(jnp.int32, (N_GROUPS, SUBLANES, RG_WIDTH), 1)
    ti = lax.broadcasted_iota(jnp.int32, (CHUNK, CHUNK), 0)
    si = lax.broadcasted_iota(jnp.int32, (CHUNK, CHUNK), 1)
    grp = SUBLANES.bit_length() - 1
    pair = jnp.where((ti >> grp) == (si >> grp), ti - si, -1)
    h = SUBLANES
    while h < CHUNK:
        shift = h.bit_length() - 1
        pair = jnp.where((((ti ^ si) >> shift) == 1) & (ti > si), SUBLANES + shift - grp, pair)
        h *= 2
    pair_s[...] = pair

    for n in range(nb):
        for c in range(t // CHUNK):
            rows = pl.ds(n * t + c * CHUNK, CHUNK)
            for hd in range(N_HEADS):
                c0 = hd * HEAD_DIM
                cols = slice(c0, c0 + HEAD_DIM)
                hb = hd * HEAD_COLS
                q = _silu(proj_s[rows, hb + Q_OFF:hb + Q_OFF + HEAD_DIM])
                kt = fg_half[:, cols] * jnp.tanh(
                    0.5 * proj_s[rows, hb + F_OFF:hb + F_OFF + HEAD_DIM])
                fg = fg_mid[:, cols] + kt
                k = fg_half[:, cols] - kt
                v = proj_s[rows, hb + V_OFF:hb + V_OFF + HEAD_DIM]
                og = proj_s[rows, hb + G_OFF:hb + G_OFF + HEAD_DIM]
                o, st_new = _hgrn2_chunk(q, fg, k, v, st_s[n, hd], r8, pair_s)
                st_s[n, hd] = st_new
                o = _rms(o) * gain[:, cols] * _silu(og)
                omix_s[rows, c0:c0 + HEAD_DIM] = o.astype(BF16)
            xc = xc_s[rows, :]
            half_z = (1.0 + jnp.tanh(0.5 * gate_s[rows, 0:RG_WIDTH])) * quarter_rate
            tp = jnp.tanh(half_z)
            w = 1.0 / (1.0 - tp)
            a = (1.0 + tp) * w
            ig2 = 1.0 + jnp.tanh(0.5 * gate_s[rows, RG_WIDTH:2 * RG_WIDTH])
            u = (jnp.sqrt(-tp) * w) * (ig2 * xc)
            ca = a.reshape(N_GROUPS, SUBLANES, RG_WIDTH)
            cu = u.reshape(N_GROUPS, SUBLANES, RG_WIDTH)
            for sh in (1, 2, 4):
                keep = r8w >= sh
                cu = jnp.where(keep, ca * pltpu.roll(cu, sh, 1) + cu, cu)
                ca = jnp.where(keep, ca * pltpu.roll(ca, sh, 1), ca)
            hprev = hc_s[n]
            hs = []
            for g in range(N_GROUPS):
                hg = ca[g] * hprev + cu[g]
                hs.append(hg)
                hprev = hg[SUBLANES - 1:SUBLANES, :]
            hc_s[n] = hprev
            hseq = jnp.concatenate(hs, axis=0)
            gr = proj_s[rows, GR_COL:IN_WIDTH]
            omix_s[rows, HG_WIDTH:HG_WIDTH + RG_WIDTH] = (hseq * _gelu_tanh(gr)).astype(BF16)

    y = _dot(omix_s[...], wout_s[...])
    for n in range(nb):
        g1 = mod_ref[n][:, 2 * D_MODEL:3 * D_MODEL]
        x1_ref[n] = x_ref[n] + g1 * y[n * t:(n + 1) * t, :]

    @pl.when(i == last)
    def _():
        for n in range(nb):
            for h in range(N_HEADS):
                s_ref[n, h] = st_s[n, h].T
            h_ref[n] = hc_s[n]
            cbo_ref[n] = xr_s[n, pad + t - tail:pad + t, :]

    for n in range(nb):
        xr_s[n, pad - tail:pad, :] = xr_s[n, pad + t - tail:pad + t, :]


def _const_spec(shape):
    zeros = (0,) * len(shape)
    return pl.BlockSpec(shape, lambda b, i: zeros, pipeline_mode=pl.Buffered(1))


def _mixer(x, mod, s0, h0, cb0, lbl, w_in, gain, conv_w, conv_b, w_rg, b_rg, lam, w_out, *, nb, t):
    bsz, seq, _ = x.shape
    grid = (bsz // nb, seq // t)
    rows = nb * t
    kern = functools.partial(_mixer_kernel, nb=nb, t=t)
    seq_spec = pl.BlockSpec((nb, t, D_MODEL), lambda b, i: (b, i, 0))
    in_specs = [
        seq_spec,
        pl.BlockSpec((nb, 1, N_MOD * D_MODEL), lambda b, i: (b, 0, 0)),
        pl.BlockSpec((nb, N_HEADS, HEAD_DIM, HEAD_DIM), lambda b, i: (b, 0, 0, 0)),
        pl.BlockSpec((nb, 1, RG_WIDTH), lambda b, i: (b, 0, 0)),
        pl.BlockSpec((nb, CONV_WIDTH - 1, RG_WIDTH), lambda b, i: (b, 0, 0)),
        _const_spec((2, HG_WIDTH)),
        _const_spec((D_MODEL, IN_WIDTH)),
        _const_spec((1, HG_WIDTH)),
        _const_spec((CONV_WIDTH, RG_WIDTH)),
        _const_spec((1, RG_WIDTH)),
        _const_spec((RG_WIDTH, 2 * RG_WIDTH)),
        _const_spec((1, 2 * RG_WIDTH)),
        _const_spec((1, RG_WIDTH)),
        _const_spec((D_MODEL, D_MODEL)),
    ]
    out_specs = [
        seq_spec,
        pl.BlockSpec((nb, N_HEADS, HEAD_DIM, HEAD_DIM), lambda b, i: (b, 0, 0, 0)),
        pl.BlockSpec((nb, 1, RG_WIDTH), lambda b, i: (b, 0, 0)),
        pl.BlockSpec((nb, CONV_WIDTH - 1, RG_WIDTH), lambda b, i: (b, 0, 0)),
    ]
    out_shape = [
        jax.ShapeDtypeStruct((bsz, seq, D_MODEL), F32),
        jax.ShapeDtypeStruct((bsz, N_HEADS, HEAD_DIM, HEAD_DIM), F32),
        jax.ShapeDtypeStruct((bsz, 1, RG_WIDTH), F32),
        jax.ShapeDtypeStruct((bsz, CONV_WIDTH - 1, RG_WIDTH), F32),
    ]
    scratch = [
        pltpu.VMEM((rows, D_MODEL), BF16),
        pltpu.VMEM((rows, IN_WIDTH), F32),
        pltpu.VMEM((nb, SUBLANES + t, RG_WIDTH), F32),
        pltpu.VMEM((rows, 2 * RG_WIDTH), F32),
        pltpu.VMEM((rows, RG_WIDTH), F32),
        pltpu.VMEM((rows, D_MODEL), BF16),
        pltpu.VMEM((nb, N_HEADS, HEAD_DIM, HEAD_DIM), F32),
        pltpu.VMEM((nb, 1, RG_WIDTH), F32),
        pltpu.VMEM((D_MODEL, IN_WIDTH), BF16),
        pltpu.VMEM((RG_WIDTH, 2 * RG_WIDTH), BF16),
        pltpu.VMEM((D_MODEL, D_MODEL), BF16),
        pltpu.VMEM((CHUNK, CHUNK), jnp.int32),
    ]
    return pl.pallas_call(
        kern,
        grid=grid,
        in_specs=in_specs,
        out_specs=out_specs,
        out_shape=out_shape,
        scratch_shapes=scratch,
        compiler_params=pltpu.CompilerParams(
            dimension_semantics=("arbitrary", "arbitrary"), vmem_limit_bytes=VMEM_LIMIT),
        name="mixer",
    )(x, mod, s0, h0, cb0, lbl, w_in, gain, conv_w, conv_b, w_rg, b_rg, lam, w_out)


def _mlp_kernel(x1_ref, mod_ref, wup_ref, wdn_ref, fgain_ref, y_ref, hn_s, *, nb, t):
    for n in range(nb):
        m = mod_ref[n]
        sh2 = m[:, 3 * D_MODEL:4 * D_MODEL]
        sc2 = m[:, 4 * D_MODEL:5 * D_MODEL]
        hn_s[n * t:(n + 1) * t, :] = (_rms(x1_ref[n]) * (1.0 + sc2) + sh2).astype(BF16)
    hn = hn_s[...]
    acc = None
    for j in range(D_FF // D_MODEL):
        up = _dot(hn, wup_ref[:, j * D_MODEL:(j + 1) * D_MODEL])
        act = jnp.square(jnp.maximum(up, 0.0)).astype(BF16)
        dn = _dot(act, wdn_ref[j * D_MODEL:(j + 1) * D_MODEL, :])
        acc = dn if acc is None else acc + dn
    for n in range(nb):
        g2 = mod_ref[n][:, 5 * D_MODEL:6 * D_MODEL]
        x2 = x1_ref[n] + g2 * acc[n * t:(n + 1) * t, :]
        y_ref[n] = _rms(x2) * fgain_ref[...]


def _mlp(x1, mod, w_up, w_down, fgain, *, nb, t):
    bsz, seq, _ = x1.shape
    grid = (bsz // nb, seq // t)
    kern = functools.partial(_mlp_kernel, nb=nb, t=t)
    seq_spec = pl.BlockSpec((nb, t, D_MODEL), lambda b, i: (b, i, 0))
    return pl.pallas_call(
        kern,
        grid=grid,
        in_specs=[
            seq_spec,
            pl.BlockSpec((nb, 1, N_MOD * D_MODEL), lambda b, i: (b, 0, 0)),
            _const_spec((D_MODEL, D_FF)),
            _const_spec((D_FF, D_MODEL)),
            _const_spec((1, D_MODEL)),
        ],
        out_specs=seq_spec,
        out_shape=jax.ShapeDtypeStruct((bsz, seq, D_MODEL), F32),
        scratch_shapes=[pltpu.VMEM((nb * t, D_MODEL), BF16)],
        compiler_params=pltpu.CompilerParams(
            dimension_semantics=("arbitrary", "arbitrary"), vmem_limit_bytes=VMEM_LIMIT),
        name="mlp",
    )(x1, mod, w_up, w_down, fgain)


def _block_diag(w):
    n, c, d = w.shape
    eye = jnp.eye(n, dtype=w.dtype)
    return (eye[:, None, :, None] * w[:, :, None, :]).reshape(n * c, n * d)


def kernel(x_prompt, x_sample, c_prompt, c_sample, state_hgrn, state_rglru, cache_conv,
           hg_lb_logits, w_ada, b_ada, w_in, hg_norm_gain, conv_w, conv_b,
           rg_wa, rg_ba, rg_wx, rg_bx, rg_lambda, w_out, w_up, w_down, final_gain):
    bp = x_prompt.shape[0]
    bs = x_sample.shape[0]

    w_hg = w_in[0][:, :4 * HG_WIDTH].reshape(D_MODEL, 4, N_HEADS, HEAD_DIM)
    w_hg = w_hg.transpose(0, 2, 1, 3).reshape(D_MODEL, 4 * HG_WIDTH)
    w_in_b = jnp.concatenate([w_hg, w_in[0][:, 4 * HG_WIDTH:]], axis=1).astype(BF16)
    w_out_b = w_out[0].astype(BF16)
    w_up_b = w_up[0].astype(BF16)
    w_down_b = w_down[0].astype(BF16)
    w_rg = jnp.concatenate([_block_diag(rg_wa[0]), _block_diag(rg_wx[0])], axis=1).astype(BF16)
    b_rg = jnp.concatenate([rg_ba[0], rg_bx[0]])[None, :]
    gain = hg_norm_gain[0][None, :]
    cbias = conv_b[0][None, :]
    lam = rg_lambda[0][None, :]
    fgain = final_gain[None, :]

    c_all = jnp.concatenate([c_prompt, c_sample], axis=0)
    mod = _ada(c_all, w_ada[0], b_ada[0][None, :])
    mod_p = mod[:bp][:, None, :]
    mod_s = mod[bp:][:, None, :]

    def trunk(x, mod_x, s0, h0, cb0):
        nb, t = _block_rows(x.shape[0], x.shape[1])
        x1, s_new, h_new, cb_new = _mixer(
            x, mod_x, s0, h0[:, None, :], cb0, hg_lb_logits, w_in_b, gain, conv_w[0], cbias,
            w_rg, b_rg, lam, w_out_b, nb=nb, t=t)
        y = _mlp(x1, mod_x, w_up_b, w_down_b, fgain, nb=nb, t=t)
        return y, s_new[None], h_new[:, 0, :][None], cb_new[None]

    s0p = jnp.zeros((bp, N_HEADS, HEAD_DIM, HEAD_DIM), state_hgrn.dtype)
    h0p = jnp.zeros((bp, RG_WIDTH), state_rglru.dtype)
    cbp = jnp.zeros((bp, CONV_WIDTH - 1, RG_WIDTH), cache_conv.dtype)
    y_p, s_p, h_p, cb_p = trunk(x_prompt, mod_p, s0p, h0p, cbp)
    y_s, s_s, h_s, cb_s = trunk(x_sample, mod_s, state_hgrn[0], state_rglru[0], cache_conv[0])
    return (y_p, y_s, s_p, h_p, cb_p, s_s, h_s, cb_s)
```

```python
import functools

import jax
import jax.numpy as jnp
from jax import lax
from jax.experimental import pallas as pl
from jax.experimental.pallas import tpu as pltpu

F32 = jnp.float32
BF16 = jnp.bfloat16

D_MODEL = 1024
HG_WIDTH = 512
RG_WIDTH = 512
HEAD_DIM = 128
N_HEADS = 4
N_MOD = 6
D_FF = 4096
IN_WIDTH = 4 * HG_WIDTH + 2 * RG_WIDTH
HEAD_COLS = 4 * HEAD_DIM
Q_OFF, F_OFF, V_OFF, G_OFF = 0, HEAD_DIM, 2 * HEAD_DIM, 3 * HEAD_DIM
XR_COL = N_HEADS * HEAD_COLS
GR_COL = XR_COL + RG_WIDTH
CONV_WIDTH = 4
RG_C = 8.0
EPS = 1e-6

SUBLANES = 8
MXU_DIM = 256
MAX_CHUNK = 128
VMEM_LIMIT = 52 * 1024 * 1024
MAX_SEQ_ROWS = 512
MIN_STEP_ROWS = 512


def _block_rows(bsz, seq):
    t = min(seq, MAX_SEQ_ROWS)
    assert seq % t == 0 and t % min(t, MAX_CHUNK) == 0
    nb = 1
    if t < MAX_SEQ_ROWS:
        nb = min(bsz, MIN_STEP_ROWS // t)
        while bsz % nb:
            nb -= 1
    return nb, t


def _sigmoid(x):
    return 0.5 * jnp.tanh(0.5 * x) + 0.5


def _silu(x):
    h = 0.5 * x
    return h + h * jnp.tanh(h)


def _gelu_tanh(x):
    c = 0.7978845608028654
    return 0.5 * x * (1.0 + jnp.tanh(c * (x + 0.044715 * (x * x * x))))


def _rms(x):
    return x * lax.rsqrt(jnp.mean(x * x, axis=-1, keepdims=True) + EPS)


def _dot(a, b):
    return jnp.dot(a, b, preferred_element_type=F32)


def _dot_nt(a, b):
    return lax.dot_general(a, b, (((1,), (1,)), ((), ())), preferred_element_type=F32)


def _ada_kernel(c_ref, w_ref, b_ref, o_ref):
    c = c_ref[...]
    o_ref[...] = _dot(_silu(c).astype(BF16), w_ref[...].astype(BF16)) + b_ref[...]


def _ada(c_all, w_ada, b_ada):
    rows = c_all.shape[0]
    tn = 1536
    return pl.pallas_call(
        _ada_kernel,
        grid=(N_MOD * D_MODEL // tn,),
        in_specs=[
            pl.BlockSpec((rows, D_MODEL), lambda j: (0, 0)),
            pl.BlockSpec((D_MODEL, tn), lambda j: (0, j)),
            pl.BlockSpec((1, tn), lambda j: (0, j)),
        ],
        out_specs=pl.BlockSpec((rows, tn), lambda j: (0, j)),
        out_shape=jax.ShapeDtypeStruct((rows, N_MOD * D_MODEL), F32),
        compiler_params=pltpu.CompilerParams(
            dimension_semantics=("arbitrary",), vmem_limit_bytes=VMEM_LIMIT),
        name="ada_mod",
    )(c_all, w_ada, b_ada)


def _group_scan_products(f3, r8):
    pf = f3
    for sh in (1, 2, 4):
        pf = jnp.where(r8 >= sh, pf * pltpu.roll(pf, sh, 1), pf)
    pb = jnp.where(r8 < SUBLANES - 1, pltpu.roll(f3, SUBLANES - 1, 1), 1.0)
    for sh in (1, 2, 4):
        pb = jnp.where(r8 < SUBLANES - sh, pb * pltpu.roll(pb, SUBLANES - sh, 1), pb)
    return pf, pb


def _block_products(pf8, pb8):
    n_groups = len(pf8)
    out = {SUBLANES: (pf8, pb8)}
    curf, curb = pf8, pb8
    h = SUBLANES
    while h < n_groups * SUBLANES:
        g = h // SUBLANES
        nf, nb = [], []
        for grp in range(n_groups):
            base = (grp // (2 * g)) * 2 * g
            if grp % (2 * g) >= g:
                total_lower = curf[base + g - 1][SUBLANES - 1:SUBLANES, :]
                nf.append(curf[grp] * total_lower)
                nb.append(curb[grp])
            else:
                total_upper = curf[base + 2 * g - 1][SUBLANES - 1:SUBLANES, :]
                nf.append(curf[grp])
                nb.append(curb[grp] * total_upper)
        curf, curb = nf, nb
        h *= 2
        out[h] = (curf, curb)
    return out


def _hgrn2_chunk(q, fg, k, v, st, r8, pair_ref):
    chunk = q.shape[0]
    n_groups = chunk // SUBLANES
    q3 = q.reshape(n_groups, SUBLANES, HEAD_DIM)
    f3 = fg.reshape(n_groups, SUBLANES, HEAD_DIM)
    k3 = k.reshape(n_groups, SUBLANES, HEAD_DIM)

    s = jnp.sum(q3 * k3, axis=-1, keepdims=True).reshape(chunk, 1)
    scores = jnp.where(pair_ref[...] == 0, s, 0.0)
    kd = k3
    for delta in range(1, SUBLANES):
        kd = f3 * pltpu.roll(kd, 1, 1)
        s = jnp.sum(q3 * kd, axis=-1, keepdims=True).reshape(chunk, 1)
        scores = jnp.where(pair_ref[...] == delta, s, scores)

    pf8, pb8 = _group_scan_products(f3, r8)
    prods = _block_products([pf8[g] for g in range(n_groups)],
                            [pb8[g] for g in range(n_groups)])
    qg = [q3[g] for g in range(n_groups)]
    kg = [k3[g] for g in range(n_groups)]

    def scaled(tiles, facs):
        return jnp.concatenate([t * p for t, p in zip(tiles, facs)], axis=0).astype(BF16)

    h = SUBLANES
    level = SUBLANES
    while h < chunk:
        a = _dot_nt(scaled(qg, prods[h][0]), scaled(kg, prods[h][1]))
        scores = jnp.where(pair_ref[...] == level, a, scores)
        h *= 2
        level += 1
    o = _dot(scores.astype(BF16), v.astype(BF16))

    o = o + _dot_nt(scaled(qg, prods[chunk][0]), st.astype(BF16))
    total = prods[chunk][0][n_groups - 1][SUBLANES - 1:SUBLANES, :]
    st_new = st * total + _dot(v.T.astype(BF16), scaled(kg, prods[chunk][1]))
    return o, st_new


def _mixer_kernel(x_ref, mod_ref, s0_ref, h0_ref, cb0_ref, lbl_ref, win_ref, gain_ref,
                  cw_ref, cbias_ref, wrg_ref, brg_ref, lam_ref, wout_ref,
                  x1_ref, s_ref, h_ref, cbo_ref,
                  hn_s, proj_s, xr_s, gate_s, xc_s, omix_s, st_s, hc_s,
                  win_s, wrg_s, wout_s, pair_s, *, nb, t, chunk):
    i = pl.program_id(1)
    last = pl.num_programs(1) - 1
    tail = CONV_WIDTH - 1
    pad = SUBLANES
    n_groups = chunk // SUBLANES

    @pl.when((pl.program_id(0) == 0) & (i == 0))
    def _():
        for kind in range(4):
            for h in range(N_HEADS):
                src = kind * HG_WIDTH + h * HEAD_DIM
                dst = h * HEAD_COLS + kind * HEAD_DIM
                win_s[:, dst:dst + HEAD_DIM] = win_ref[:, src:src + HEAD_DIM]
        win_s[:, XR_COL:IN_WIDTH] = win_ref[:, XR_COL:IN_WIDTH]
        wrg_s[...] = wrg_ref[...]
        wout_s[...] = wout_ref[...]

    @pl.when(i == 0)
    def _():
        for n in range(nb):
            for h in range(N_HEADS):
                st_s[n, h] = s0_ref[n, h].T
            hc_s[n] = h0_ref[n]
            xr_s[n, 0:pad, :] = jnp.zeros((pad, RG_WIDTH), F32)
            xr_s[n, pad - tail:pad, :] = cb0_ref[n]

    for n in range(nb):
        m = mod_ref[n]
        sh1 = m[:, 0:D_MODEL]
        sc1 = m[:, D_MODEL:2 * D_MODEL]
        hn = _rms(x_ref[n]) * (1.0 + sc1) + sh1
        hn_s[n * t:(n + 1) * t, :] = hn.astype(BF16)

    def project(c0, c1):
        proj_s[:, c0:c1] = _dot(hn_s[...], win_s[:, c0:c1])

    project(XR_COL, GR_COL)

    cw = cw_ref[...]
    for n in range(nb):
        xr_s[n, pad:pad + t, :] = proj_s[n * t:(n + 1) * t, XR_COL:GR_COL]
        xc = cbias_ref[...] + xr_s[n, pad - tail:pad - tail + t, :] * cw[0:1, :]
        for j in range(1, CONV_WIDTH):
            xc = xc + xr_s[n, pad - tail + j:pad - tail + j + t, :] * cw[j:j + 1, :]
        xc_s[n * t:(n + 1) * t, :] = xc
    project(0, 2 * HEAD_COLS)
    xcb = xc_s[...].astype(BF16)
    brg = brg_ref[...]
    for p in range(RG_WIDTH // MXU_DIM):
        lo, hi = p * MXU_DIM, (p + 1) * MXU_DIM
        g = _dot(xcb[:, lo:hi], wrg_s[p])
        gate_s[:, lo:hi] = g[:, :MXU_DIM] + brg[:, lo:hi]
        gate_s[:, RG_WIDTH + lo:RG_WIDTH + hi] = g[:, MXU_DIM:] + brg[:, RG_WIDTH + lo:RG_WIDTH + hi]
    project(2 * HEAD_COLS, XR_COL)
    project(GR_COL, IN_WIDTH)

    lbl = lbl_ref[...]
    lmax = jnp.maximum(lbl[0:1, :], lbl[1:2, :])
    e0 = jnp.exp(lbl[0:1, :] - lmax)
    e1 = jnp.exp(lbl[1:2, :] - lmax)
    lb = e0 / (e0 + e1)
    fg_mid = 0.5 * (1.0 + lb)
    fg_half = 0.5 * (1.0 - lb)
    nlam = -lam_ref[...]
    ez = jnp.exp(-jnp.abs(nlam))
    one_p = 1.0 + ez
    log1p_ez = jnp.where(one_p == 1.0, ez, jnp.log(one_p) * (ez / (one_p - 1.0)))
    softplus_nlam = jnp.maximum(nlam, 0.0) + log1p_ez
    quarter_rate = (-0.25 * RG_C) * softplus_nlam
    gain = gain_ref[...]

    r8 = lax.broadcasted_iota(jnp.int32, (n_groups, SUBLANES, HEAD_DIM), 1)
    r8w = lax.broadcasted_iota(jnp.int32, (n_groups, SUBLANES, RG_WIDTH), 1)
    ti = lax.broadcasted_iota(jnp.int32, (chunk, chunk), 0)
    si = lax.broadcasted_iota(jnp.int32, (chunk, chunk), 1)
    grp = SUBLANES.bit_length() - 1
    pair = jnp.where((ti >> grp) == (si >> grp), ti - si, -1)
    h = SUBLANES
    while h < chunk:
        shift = h.bit_length() - 1
        pair = jnp.where((((ti ^ si) >> shift) == 1) & (ti > si), SUBLANES + shift - grp, pair)
        h *= 2
    pair_s[...] = pair

    for n in range(nb):
        for c in range(t // chunk):
            rows = pl.ds(n * t + c * chunk, chunk)
            for hd in range(N_HEADS):
                c0 = hd * HEAD_DIM
                cols = slice(c0, c0 + HEAD_DIM)
                hb = hd * HEAD_COLS
                q = _silu(proj_s[rows, hb + Q_OFF:hb + Q_OFF + HEAD_DIM])
                kt = fg_half[:, cols] * jnp.tanh(
                    0.5 * proj_s[rows, hb + F_OFF:hb + F_OFF + HEAD_DIM])
                fg = fg_mid[:, cols] + kt
                k = fg_half[:, cols] - kt
                v = proj_s[rows, hb + V_OFF:hb + V_OFF + HEAD_DIM]
                og = proj_s[rows, hb + G_OFF:hb + G_OFF + HEAD_DIM]
                o, st_new = _hgrn2_chunk(q, fg, k, v, st_s[n, hd], r8, pair_s)
                st_s[n, hd] = st_new
                o = _rms(o) * gain[:, cols] * _silu(og)
                omix_s[rows, c0:c0 + HEAD_DIM] = o.astype(BF16)
            xc = xc_s[rows, :]
            half_z = (1.0 + jnp.tanh(0.5 * gate_s[rows, 0:RG_WIDTH])) * quarter_rate
            tp = jnp.tanh(half_z)
            w = 1.0 / (1.0 - tp)
            a = (1.0 + tp) * w
            ig2 = 1.0 + jnp.tanh(0.5 * gate_s[rows, RG_WIDTH:2 * RG_WIDTH])
            u = (jnp.sqrt(-tp) * w) * (ig2 * xc)
            ca = a.reshape(n_groups, SUBLANES, RG_WIDTH)
            cu = u.reshape(n_groups, SUBLANES, RG_WIDTH)
            for sh in (1, 2, 4):
                keep = r8w >= sh
                cu = jnp.where(keep, ca * pltpu.roll(cu, sh, 1) + cu, cu)
                ca = jnp.where(keep, ca * pltpu.roll(ca, sh, 1), ca)
            hprev = hc_s[n]
            hs = []
            for g in range(n_groups):
                hg = ca[g] * hprev + cu[g]
                hs.append(hg)
                hprev = hg[SUBLANES - 1:SUBLANES, :]
            hc_s[n] = hprev
            hseq = jnp.concatenate(hs, axis=0)
            gr = proj_s[rows, GR_COL:IN_WIDTH]
            omix_s[rows, HG_WIDTH:HG_WIDTH + RG_WIDTH] = (hseq * _gelu_tanh(gr)).astype(BF16)

    y = _dot(omix_s[...], wout_s[...])
    for n in range(nb):
        g1 = mod_ref[n][:, 2 * D_MODEL:3 * D_MODEL]
        x1_ref[n] = x_ref[n] + g1 * y[n * t:(n + 1) * t, :]

    @pl.when(i == last)
    def _():
        for n in range(nb):
            for h in range(N_HEADS):
                s_ref[n, h] = st_s[n, h].T
            h_ref[n] = hc_s[n]
            cbo_ref[n] = xr_s[n, pad + t - tail:pad + t, :]

    for n in range(nb):
        xr_s[n, pad - tail:pad, :] = xr_s[n, pad + t - tail:pad + t, :]


def _const_spec(shape):
    zeros = (0,) * len(shape)
    return pl.BlockSpec(shape, lambda b, i: zeros, pipeline_mode=pl.Buffered(1))


def _mixer(x, mod, s0, h0, cb0, lbl, w_in, gain, conv_w, conv_b, w_rg, b_rg, lam, w_out, *, nb, t):
    bsz, seq, _ = x.shape
    grid = (bsz // nb, seq // t)
    rows = nb * t
    chunk = min(t, MAX_CHUNK)
    kern = functools.partial(_mixer_kernel, nb=nb, t=t, chunk=chunk)
    seq_spec = pl.BlockSpec((nb, t, D_MODEL), lambda b, i: (b, i, 0))
    in_specs = [
        seq_spec,
        pl.BlockSpec((nb, 1, N_MOD * D_MODEL), lambda b, i: (b, 0, 0)),
        pl.BlockSpec((nb, N_HEADS, HEAD_DIM, HEAD_DIM), lambda b, i: (b, 0, 0, 0)),
        pl.BlockSpec((nb, 1, RG_WIDTH), lambda b, i: (b, 0, 0)),
        pl.BlockSpec((nb, CONV_WIDTH - 1, RG_WIDTH), lambda b, i: (b, 0, 0)),
        _const_spec((2, HG_WIDTH)),
        _const_spec((D_MODEL, IN_WIDTH)),
        _const_spec((1, HG_WIDTH)),
        _const_spec((CONV_WIDTH, RG_WIDTH)),
        _const_spec((1, RG_WIDTH)),
        _const_spec((RG_WIDTH // MXU_DIM, MXU_DIM, 2 * MXU_DIM)),
        _const_spec((1, 2 * RG_WIDTH)),
        _const_spec((1, RG_WIDTH)),
        _const_spec((D_MODEL, D_MODEL)),
    ]
    out_specs = [
        seq_spec,
        pl.BlockSpec((nb, N_HEADS, HEAD_DIM, HEAD_DIM), lambda b, i: (b, 0, 0, 0)),
        pl.BlockSpec((nb, 1, RG_WIDTH), lambda b, i: (b, 0, 0)),
        pl.BlockSpec((nb, CONV_WIDTH - 1, RG_WIDTH), lambda b, i: (b, 0, 0)),
    ]
    out_shape = [
        jax.ShapeDtypeStruct((bsz, seq, D_MODEL), F32),
        jax.ShapeDtypeStruct((bsz, N_HEADS, HEAD_DIM, HEAD_DIM), F32),
        jax.ShapeDtypeStruct((bsz, 1, RG_WIDTH), F32),
        jax.ShapeDtypeStruct((bsz, CONV_WIDTH - 1, RG_WIDTH), F32),
    ]
    scratch = [
        pltpu.VMEM((rows, D_MODEL), BF16),
        pltpu.VMEM((rows, IN_WIDTH), F32),
        pltpu.VMEM((nb, SUBLANES + t, RG_WIDTH), F32),
        pltpu.VMEM((rows, 2 * RG_WIDTH), F32),
        pltpu.VMEM((rows, RG_WIDTH), F32),
        pltpu.VMEM((rows, D_MODEL), BF16),
        pltpu.VMEM((nb, N_HEADS, HEAD_DIM, HEAD_DIM), F32),
        pltpu.VMEM((nb, 1, RG_WIDTH), F32),
        pltpu.VMEM((D_MODEL, IN_WIDTH), BF16),
        pltpu.VMEM((RG_WIDTH // MXU_DIM, MXU_DIM, 2 * MXU_DIM), BF16),
        pltpu.VMEM((D_MODEL, D_MODEL), BF16),
        pltpu.VMEM((chunk, chunk), jnp.int32),
    ]
    return pl.pallas_call(
        kern,
        grid=grid,
        in_specs=in_specs,
        out_specs=out_specs,
        out_shape=out_shape,
        scratch_shapes=scratch,
        compiler_params=pltpu.CompilerParams(
            dimension_semantics=("arbitrary", "arbitrary"), vmem_limit_bytes=VMEM_LIMIT),
        name="mixer",
    )(x, mod, s0, h0, cb0, lbl, w_in, gain, conv_w, conv_b, w_rg, b_rg, lam, w_out)


def _mlp_kernel(x1_ref, mod_ref, wup_ref, wdn_ref, fgain_ref, y_ref, hn_s, *, nb, t):
    for n in range(nb):
        m = mod_ref[n]
        sh2 = m[:, 3 * D_MODEL:4 * D_MODEL]
        sc2 = m[:, 4 * D_MODEL:5 * D_MODEL]
        hn_s[n * t:(n + 1) * t, :] = (_rms(x1_ref[n]) * (1.0 + sc2) + sh2).astype(BF16)
    hn = hn_s[...]
    acc = None
    for j in range(D_FF // D_MODEL):
        up = _dot(hn, wup_ref[:, j * D_MODEL:(j + 1) * D_MODEL])
        act = jnp.square(jnp.maximum(up, 0.0)).astype(BF16)
        dn = _dot(act, wdn_ref[j * D_MODEL:(j + 1) * D_MODEL, :])
        acc = dn if acc is None else acc + dn
    for n in range(nb):
        g2 = mod_ref[n][:, 5 * D_MODEL:6 * D_MODEL]
        x2 = x1_ref[n] + g2 * acc[n * t:(n + 1) * t, :]
        y_ref[n] = _rms(x2) * fgain_ref[...]


def _mlp(x1, mod, w_up, w_down, fgain, *, nb, t):
    bsz, seq, _ = x1.shape
    grid = (bsz // nb, seq // t)
    kern = functools.partial(_mlp_kernel, nb=nb, t=t)
    seq_spec = pl.BlockSpec((nb, t, D_MODEL), lambda b, i: (b, i, 0))
    return pl.pallas_call(
        kern,
        grid=grid,
        in_specs=[
            seq_spec,
            pl.BlockSpec((nb, 1, N_MOD * D_MODEL), lambda b, i: (b, 0, 0)),
            _const_spec((D_MODEL, D_FF)),
            _const_spec((D_FF, D_MODEL)),
            _const_spec((1, D_MODEL)),
        ],
        out_specs=seq_spec,
        out_shape=jax.ShapeDtypeStruct((bsz, seq, D_MODEL), F32),
        scratch_shapes=[pltpu.VMEM((nb * t, D_MODEL), BF16)],
        compiler_params=pltpu.CompilerParams(
            dimension_semantics=("arbitrary", "arbitrary"), vmem_limit_bytes=VMEM_LIMIT),
        name="mlp",
    )(x1, mod, w_up, w_down, fgain)


def _block_diag(w):
    n, c, d = w.shape
    eye = jnp.eye(n, dtype=w.dtype)
    return (eye[:, None, :, None] * w[:, :, None, :]).reshape(n * c, n * d)


def kernel(x_prompt, x_sample, c_prompt, c_sample, state_hgrn, state_rglru, cache_conv,
           hg_lb_logits, w_ada, b_ada, w_in, hg_norm_gain, conv_w, conv_b,
           rg_wa, rg_ba, rg_wx, rg_bx, rg_lambda, w_out, w_up, w_down, final_gain):
    bp = x_prompt.shape[0]
    bs = x_sample.shape[0]

    w_in_b = w_in[0].astype(BF16)
    w_out_b = w_out[0].astype(BF16)
    w_up_b = w_up[0].astype(BF16)
    w_down_b = w_down[0].astype(BF16)
    per = MXU_DIM // rg_wa.shape[2]
    w_rg = jnp.stack([
        jnp.concatenate([_block_diag(rg_wa[0][p * per:(p + 1) * per]),
                         _block_diag(rg_wx[0][p * per:(p + 1) * per])], axis=1)
        for p in range(RG_WIDTH // MXU_DIM)]).astype(BF16)
    b_rg = jnp.concatenate([rg_ba[0], rg_bx[0]])[None, :]
    gain = hg_norm_gain[0][None, :]
    cbias = conv_b[0][None, :]
    lam = rg_lambda[0][None, :]
    fgain = final_gain[None, :]

    c_all = jnp.concatenate([c_prompt, c_sample], axis=0)
    mod = _ada(c_all, w_ada[0], b_ada[0][None, :])
    mod_p = mod[:bp][:, None, :]
    mod_s = mod[bp:][:, None, :]

    def trunk(x, mod_x, s0, h0, cb0):
        nb, t = _block_rows(x.shape[0], x.shape[1])
        x1, s_new, h_new, cb_new = _mixer(
            x, mod_x, s0, h0[:, None, :], cb0, hg_lb_logits, w_in_b, gain, conv_w[0], cbias,
            w_rg, b_rg, lam, w_out_b, nb=nb, t=t)
        y = _mlp(x1, mod_x, w_up_b, w_down_b, fgain, nb=nb, t=t)
        return y, s_new[None], h_new[:, 0, :][None], cb_new[None]

    s0p = jnp.zeros((bp, N_HEADS, HEAD_DIM, HEAD_DIM), state_hgrn.dtype)
    h0p = jnp.zeros((bp, RG_WIDTH), state_rglru.dtype)
    cbp = jnp.zeros((bp, CONV_WIDTH - 1, RG_WIDTH), cache_conv.dtype)
    y_p, s_p, h_p, cb_p = trunk(x_prompt, mod_p, s0p, h0p, cbp)
    y_s, s_s, h_s, cb_s = trunk(x_sample, mod_s, state_hgrn[0], state_rglru[0], cache_conv[0])
    return (y_p, y_s, s_p, h_p, cb_p, s_s, h_s, cb_s)
```

```python
import functools

import jax
import jax.numpy as jnp
from jax import lax
from jax.experimental import pallas as pl
from jax.experimental.pallas import tpu as pltpu

F32 = jnp.float32
BF16 = jnp.bfloat16

D_MODEL = 1024
HG_WIDTH = 512
RG_WIDTH = 512
HEAD_DIM = 128
N_HEADS = 4
N_MOD = 6
D_FF = 4096
IN_WIDTH = 4 * HG_WIDTH + 2 * RG_WIDTH
HEAD_COLS = 4 * HEAD_DIM
Q_OFF, F_OFF, V_OFF, G_OFF = 0, HEAD_DIM, 2 * HEAD_DIM, 3 * HEAD_DIM
XR_COL = N_HEADS * HEAD_COLS
GR_COL = XR_COL + RG_WIDTH
CONV_WIDTH = 4
RG_C = 8.0
EPS = 1e-6

SUBLANES = 8
MXU_DIM = 256
MAX_CHUNK = 128
VMEM_LIMIT = 52 * 1024 * 1024
MAX_SEQ_ROWS = 512
MIN_STEP_ROWS = 512


def _block_rows(bsz, seq):
    t = min(seq, MAX_SEQ_ROWS)
    assert seq % t == 0 and t % min(t, MAX_CHUNK) == 0
    nb = 1
    if t < MAX_SEQ_ROWS:
        nb = min(bsz, MIN_STEP_ROWS // t)
        while bsz % nb:
            nb -= 1
    return nb, t


def _sigmoid(x):
    return 0.5 * jnp.tanh(0.5 * x) + 0.5


def _silu(x):
    h = 0.5 * x
    return h + h * jnp.tanh(h)


def _gelu_tanh(x):
    c = 0.7978845608028654
    return 0.5 * x * (1.0 + jnp.tanh(c * (x + 0.044715 * (x * x * x))))


def _rms(x):
    return x * lax.rsqrt(jnp.mean(x * x, axis=-1, keepdims=True) + EPS)


def _dot(a, b):
    return jnp.dot(a, b, preferred_element_type=F32)


def _dot_nt(a, b):
    return lax.dot_general(a, b, (((1,), (1,)), ((), ())), preferred_element_type=F32)


def _ada_kernel(c_ref, w_ref, b_ref, o_ref):
    c = c_ref[...]
    o_ref[...] = _dot(_silu(c).astype(BF16), w_ref[...].astype(BF16)) + b_ref[...]


def _ada(c_all, w_ada, b_ada):
    rows = c_all.shape[0]
    tn = 1536
    return pl.pallas_call(
        _ada_kernel,
        grid=(N_MOD * D_MODEL // tn,),
        in_specs=[
            pl.BlockSpec((rows, D_MODEL), lambda j: (0, 0)),
            pl.BlockSpec((D_MODEL, tn), lambda j: (0, j)),
            pl.BlockSpec((1, tn), lambda j: (0, j)),
        ],
        out_specs=pl.BlockSpec((rows, tn), lambda j: (0, j)),
        out_shape=jax.ShapeDtypeStruct((rows, N_MOD * D_MODEL), F32),
        compiler_params=pltpu.CompilerParams(
            dimension_semantics=("arbitrary",), vmem_limit_bytes=VMEM_LIMIT),
        name="ada_mod",
    )(c_all, w_ada, b_ada)


def _group_scan_products(f3, r8):
    pf = f3
    for sh in (1, 2, 4):
        pf = jnp.where(r8 >= sh, pf * pltpu.roll(pf, sh, 1), pf)
    pb = jnp.where(r8 < SUBLANES - 1, pltpu.roll(f3, SUBLANES - 1, 1), 1.0)
    for sh in (1, 2, 4):
        pb = jnp.where(r8 < SUBLANES - sh, pb * pltpu.roll(pb, SUBLANES - sh, 1), pb)
    return pf, pb


def _block_products(pf8, pb8):
    n_groups = len(pf8)
    out = {SUBLANES: (pf8, pb8)}
    curf, curb = pf8, pb8
    h = SUBLANES
    while h < n_groups * SUBLANES:
        g = h // SUBLANES
        nf, nb = [], []
        for grp in range(n_groups):
            base = (grp // (2 * g)) * 2 * g
            if grp % (2 * g) >= g:
                total_lower = curf[base + g - 1][SUBLANES - 1:SUBLANES, :]
                nf.append(curf[grp] * total_lower)
                nb.append(curb[grp])
            else:
                total_upper = curf[base + 2 * g - 1][SUBLANES - 1:SUBLANES, :]
                nf.append(curf[grp])
                nb.append(curb[grp] * total_upper)
        curf, curb = nf, nb
        h *= 2
        out[h] = (curf, curb)
    return out


def _hgrn2_chunk(q, fg, k, v, st, r8, pair_ref):
    chunk = q.shape[0]
    n_groups = chunk // SUBLANES
    q3 = q.reshape(n_groups, SUBLANES, HEAD_DIM)
    f3 = fg.reshape(n_groups, SUBLANES, HEAD_DIM)
    k3 = k.reshape(n_groups, SUBLANES, HEAD_DIM)

    s = jnp.sum(q3 * k3, axis=-1, keepdims=True).reshape(chunk, 1)
    scores = jnp.where(pair_ref[...] == 0, s, 0.0)
    kd = k3
    for delta in range(1, SUBLANES):
        kd = f3 * pltpu.roll(kd, 1, 1)
        s = jnp.sum(q3 * kd, axis=-1, keepdims=True).reshape(chunk, 1)
        scores = jnp.where(pair_ref[...] == delta, s, scores)

    pf8, pb8 = _group_scan_products(f3, r8)
    prods = _block_products([pf8[g] for g in range(n_groups)],
                            [pb8[g] for g in range(n_groups)])
    qg = [q3[g] for g in range(n_groups)]
    kg = [k3[g] for g in range(n_groups)]

    def scaled(tiles, facs):
        return jnp.concatenate([t * p for t, p in zip(tiles, facs)], axis=0).astype(BF16)

    h = SUBLANES
    level = SUBLANES
    while h < chunk:
        a = _dot_nt(scaled(qg, prods[h][0]), scaled(kg, prods[h][1]))
        scores = jnp.where(pair_ref[...] == level, a, scores)
        h *= 2
        level += 1
    o = _dot(scores.astype(BF16), v.astype(BF16))

    o = o + _dot_nt(scaled(qg, prods[chunk][0]), st.astype(BF16))
    total = prods[chunk][0][n_groups - 1][SUBLANES - 1:SUBLANES, :]
    st_new = st * total + _dot(v.T.astype(BF16), scaled(kg, prods[chunk][1]))
    return o, st_new


def _mixer_kernel(x_ref, mod_ref, s0_ref, h0_ref, cb0_ref, lbl_ref, win_ref, gain_ref,
                  cw_ref, cbias_ref, wrg_ref, brg_ref, lam_ref, wout_ref,
                  x1_ref, s_ref, h_ref, cbo_ref,
                  hn_s, proj_s, xr_s, gate_s, xc_s, omix_s, st_s, hc_s,
                  win_s, wrg_s, wout_s, pair_s, *, nb, t, chunk):
    i = pl.program_id(1)
    last = pl.num_programs(1) - 1
    tail = CONV_WIDTH - 1
    pad = SUBLANES
    n_groups = chunk // SUBLANES

    @pl.when((pl.program_id(0) == 0) & (i == 0))
    def _():
        for kind in range(4):
            for h in range(N_HEADS):
                src = kind * HG_WIDTH + h * HEAD_DIM
                dst = h * HEAD_COLS + kind * HEAD_DIM
                win_s[:, dst:dst + HEAD_DIM] = win_ref[:, src:src + HEAD_DIM]
        win_s[:, XR_COL:IN_WIDTH] = win_ref[:, XR_COL:IN_WIDTH]
        wrg_s[...] = wrg_ref[...]
        wout_s[...] = wout_ref[...]

    @pl.when(i == 0)
    def _():
        for n in range(nb):
            for h in range(N_HEADS):
                st_s[n, h] = s0_ref[n, h].T
            hc_s[n] = h0_ref[n]
            xr_s[n, 0:pad, :] = jnp.zeros((pad, RG_WIDTH), F32)
            xr_s[n, pad - tail:pad, :] = cb0_ref[n]

    for n in range(nb):
        m = mod_ref[n]
        sh1 = m[:, 0:D_MODEL]
        sc1 = m[:, D_MODEL:2 * D_MODEL]
        hn = _rms(x_ref[n]) * (1.0 + sc1) + sh1
        hn_s[n * t:(n + 1) * t, :] = hn.astype(BF16)

    def project(c0, c1):
        proj_s[:, c0:c1] = _dot(hn_s[...], win_s[:, c0:c1])

    def conv():
        cw = cw_ref[...]
        for n in range(nb):
            xr_s[n, pad:pad + t, :] = proj_s[n * t:(n + 1) * t, XR_COL:GR_COL]
            xc = cbias_ref[...] + xr_s[n, pad - tail:pad - tail + t, :] * cw[0:1, :]
            for j in range(1, CONV_WIDTH):
                xc = xc + xr_s[n, pad - tail + j:pad - tail + j + t, :] * cw[j:j + 1, :]
            xc_s[n * t:(n + 1) * t, :] = xc

    def gates():
        xcb = xc_s[...].astype(BF16)
        brg = brg_ref[...]
        for p in range(RG_WIDTH // MXU_DIM):
            lo, hi = p * MXU_DIM, (p + 1) * MXU_DIM
            g = _dot(xcb[:, lo:hi], wrg_s[p])
            gate_s[:, lo:hi] = g[:, :MXU_DIM] + brg[:, lo:hi]
            gate_s[:, RG_WIDTH + lo:RG_WIDTH + hi] = g[:, MXU_DIM:] + brg[:, RG_WIDTH + lo:RG_WIDTH + hi]

    lbl = lbl_ref[...]
    lmax = jnp.maximum(lbl[0:1, :], lbl[1:2, :])
    e0 = jnp.exp(lbl[0:1, :] - lmax)
    e1 = jnp.exp(lbl[1:2, :] - lmax)
    lb = e0 / (e0 + e1)
    fg_mid = 0.5 * (1.0 + lb)
    fg_half = 0.5 * (1.0 - lb)
    nlam = -lam_ref[...]
    ez = jnp.exp(-jnp.abs(nlam))
    one_p = 1.0 + ez
    log1p_ez = jnp.where(one_p == 1.0, ez, jnp.log(one_p) * (ez / (one_p - 1.0)))
    softplus_nlam = jnp.maximum(nlam, 0.0) + log1p_ez
    quarter_rate = (-0.25 * RG_C) * softplus_nlam
    gain = gain_ref[...]

    r8 = lax.broadcasted_iota(jnp.int32, (n_groups, SUBLANES, HEAD_DIM), 1)
    r8w = lax.broadcasted_iota(jnp.int32, (n_groups, SUBLANES, RG_WIDTH), 1)
    ti = lax.broadcasted_iota(jnp.int32, (chunk, chunk), 0)
    si = lax.broadcasted_iota(jnp.int32, (chunk, chunk), 1)
    grp = SUBLANES.bit_length() - 1
    pair = jnp.where((ti >> grp) == (si >> grp), ti - si, -1)
    h = SUBLANES
    while h < chunk:
        shift = h.bit_length() - 1
        pair = jnp.where((((ti ^ si) >> shift) == 1) & (ti > si), SUBLANES + shift - grp, pair)
        h *= 2
    pair_s[...] = pair

    def hgrn_head(n, c, hd):
        rows = pl.ds(n * t + c * chunk, chunk)
        c0 = hd * HEAD_DIM
        cols = slice(c0, c0 + HEAD_DIM)
        hb = hd * HEAD_COLS
        q = _silu(proj_s[rows, hb + Q_OFF:hb + Q_OFF + HEAD_DIM])
        kt = fg_half[:, cols] * jnp.tanh(
            0.5 * proj_s[rows, hb + F_OFF:hb + F_OFF + HEAD_DIM])
        fg = fg_mid[:, cols] + kt
        k = fg_half[:, cols] - kt
        v = proj_s[rows, hb + V_OFF:hb + V_OFF + HEAD_DIM]
        og = proj_s[rows, hb + G_OFF:hb + G_OFF + HEAD_DIM]
        o, st_new = _hgrn2_chunk(q, fg, k, v, st_s[n, hd], r8, pair_s)
        st_s[n, hd] = st_new
        o = _rms(o) * gain[:, cols] * _silu(og)
        omix_s[rows, c0:c0 + HEAD_DIM] = o.astype(BF16)

    def rg_chunk(n, c):
        rows = pl.ds(n * t + c * chunk, chunk)
        xc = xc_s[rows, :]
        half_z = (1.0 + jnp.tanh(0.5 * gate_s[rows, 0:RG_WIDTH])) * quarter_rate
        tp = jnp.tanh(half_z)
        w = 1.0 / (1.0 - tp)
        a = (1.0 + tp) * w
        ig2 = 1.0 + jnp.tanh(0.5 * gate_s[rows, RG_WIDTH:2 * RG_WIDTH])
        u = (jnp.sqrt(-tp) * w) * (ig2 * xc)
        ca = a.reshape(n_groups, SUBLANES, RG_WIDTH)
        cu = u.reshape(n_groups, SUBLANES, RG_WIDTH)
        for sh in (1, 2, 4):
            keep = r8w >= sh
            cu = jnp.where(keep, ca * pltpu.roll(cu, sh, 1) + cu, cu)
            ca = jnp.where(keep, ca * pltpu.roll(ca, sh, 1), ca)
        hprev = hc_s[n]
        hs = []
        for g in range(n_groups):
            hg = ca[g] * hprev + cu[g]
            hs.append(hg)
            hprev = hg[SUBLANES - 1:SUBLANES, :]
        hc_s[n] = hprev
        hseq = jnp.concatenate(hs, axis=0)
        gr = proj_s[rows, GR_COL:IN_WIDTH]
        omix_s[rows, HG_WIDTH:HG_WIDTH + RG_WIDTH] = (hseq * _gelu_tanh(gr)).astype(BF16)

    chunks = [(n, c) for n in range(nb) for c in range(t // chunk)]
    project(0, 2 * HEAD_COLS)
    project(XR_COL, GR_COL)
    conv()
    for hd in (0, 1):
        for n, c in chunks:
            hgrn_head(n, c, hd)
    gates()
    project(2 * HEAD_COLS, XR_COL)
    project(GR_COL, IN_WIDTH)
    for hd in (2, 3):
        for n, c in chunks:
            hgrn_head(n, c, hd)
    for n, c in chunks:
        rg_chunk(n, c)


    y = _dot(omix_s[...], wout_s[...])
    for n in range(nb):
        g1 = mod_ref[n][:, 2 * D_MODEL:3 * D_MODEL]
        x1_ref[n] = x_ref[n] + g1 * y[n * t:(n + 1) * t, :]

    @pl.when(i == last)
    def _():
        for n in range(nb):
            for h in range(N_HEADS):
                s_ref[n, h] = st_s[n, h].T
            h_ref[n] = hc_s[n]
            cbo_ref[n] = xr_s[n, pad + t - tail:pad + t, :]

    for n in range(nb):
        xr_s[n, pad - tail:pad, :] = xr_s[n, pad + t - tail:pad + t, :]


def _const_spec(shape):
    zeros = (0,) * len(shape)
    return pl.BlockSpec(shape, lambda b, i: zeros, pipeline_mode=pl.Buffered(1))


def _mixer(x, mod, s0, h0, cb0, lbl, w_in, gain, conv_w, conv_b, w_rg, b_rg, lam, w_out, *, nb, t):
    bsz, seq, _ = x.shape
    grid = (bsz // nb, seq // t)
    rows = nb * t
    chunk = min(t, MAX_CHUNK)
    kern = functools.partial(_mixer_kernel, nb=nb, t=t, chunk=chunk)
    seq_spec = pl.BlockSpec((nb, t, D_MODEL), lambda b, i: (b, i, 0))
    in_specs = [
        seq_spec,
        pl.BlockSpec((nb, 1, N_MOD * D_MODEL), lambda b, i: (b, 0, 0)),
        pl.BlockSpec((nb, N_HEADS, HEAD_DIM, HEAD_DIM), lambda b, i: (b, 0, 0, 0)),
        pl.BlockSpec((nb, 1, RG_WIDTH), lambda b, i: (b, 0, 0)),
        pl.BlockSpec((nb, CONV_WIDTH - 1, RG_WIDTH), lambda b, i: (b, 0, 0)),
        _const_spec((2, HG_WIDTH)),
        _const_spec((D_MODEL, IN_WIDTH)),
        _const_spec((1, HG_WIDTH)),
        _const_spec((CONV_WIDTH, RG_WIDTH)),
        _const_spec((1, RG_WIDTH)),
        _const_spec((RG_WIDTH // MXU_DIM, MXU_DIM, 2 * MXU_DIM)),
        _const_spec((1, 2 * RG_WIDTH)),
        _const_spec((1, RG_WIDTH)),
        _const_spec((D_MODEL, D_MODEL)),
    ]
    out_specs = [
        seq_spec,
        pl.BlockSpec((nb, N_HEADS, HEAD_DIM, HEAD_DIM), lambda b, i: (b, 0, 0, 0)),
        pl.BlockSpec((nb, 1, RG_WIDTH), lambda b, i: (b, 0, 0)),
        pl.BlockSpec((nb, CONV_WIDTH - 1, RG_WIDTH), lambda b, i: (b, 0, 0)),
    ]
    out_shape = [
        jax.ShapeDtypeStruct((bsz, seq, D_MODEL), F32),
        jax.ShapeDtypeStruct((bsz, N_HEADS, HEAD_DIM, HEAD_DIM), F32),
        jax.ShapeDtypeStruct((bsz, 1, RG_WIDTH), F32),
        jax.ShapeDtypeStruct((bsz, CONV_WIDTH - 1, RG_WIDTH), F32),
    ]
    scratch = [
        pltpu.VMEM((rows, D_MODEL), BF16),
        pltpu.VMEM((rows, IN_WIDTH), F32),
        pltpu.VMEM((nb, SUBLANES + t, RG_WIDTH), F32),
        pltpu.VMEM((rows, 2 * RG_WIDTH), F32),
        pltpu.VMEM((rows, RG_WIDTH), F32),
        pltpu.VMEM((rows, D_MODEL), BF16),
        pltpu.VMEM((nb, N_HEADS, HEAD_DIM, HEAD_DIM), F32),
        pltpu.VMEM((nb, 1, RG_WIDTH), F32),
        pltpu.VMEM((D_MODEL, IN_WIDTH), BF16),
        pltpu.VMEM((RG_WIDTH // MXU_DIM, MXU_DIM, 2 * MXU_DIM), BF16),
        pltpu.VMEM((D_MODEL, D_MODEL), BF16),
        pltpu.VMEM((chunk, chunk), jnp.int32),
    ]
    return pl.pallas_call(
        kern,
        grid=grid,
        in_specs=in_specs,
        out_specs=out_specs,
        out_shape=out_shape,
        scratch_shapes=scratch,
        compiler_params=pltpu.CompilerParams(
            dimension_semantics=("arbitrary", "arbitrary"), vmem_limit_bytes=VMEM_LIMIT),
        name="mixer",
    )(x, mod, s0, h0, cb0, lbl, w_in, gain, conv_w, conv_b, w_rg, b_rg, lam, w_out)


def _mlp_kernel(x1_ref, mod_ref, wup_ref, wdn_ref, fgain_ref, y_ref, hn_s, *, nb, t):
    for n in range(nb):
        m = mod_ref[n]
        sh2 = m[:, 3 * D_MODEL:4 * D_MODEL]
        sc2 = m[:, 4 * D_MODEL:5 * D_MODEL]
        hn_s[n * t:(n + 1) * t, :] = (_rms(x1_ref[n]) * (1.0 + sc2) + sh2).astype(BF16)
    hn = hn_s[...]
    acc = None
    for j in range(D_FF // D_MODEL):
        up = _dot(hn, wup_ref[:, j * D_MODEL:(j + 1) * D_MODEL])
        act = jnp.square(jnp.maximum(up, 0.0)).astype(BF16)
        dn = _dot(act, wdn_ref[j * D_MODEL:(j + 1) * D_MODEL, :])
        acc = dn if acc is None else acc + dn
    for n in range(nb):
        g2 = mod_ref[n][:, 5 * D_MODEL:6 * D_MODEL]
        x2 = x1_ref[n] + g2 * acc[n * t:(n + 1) * t, :]
        y_ref[n] = _rms(x2) * fgain_ref[...]


def _mlp(x1, mod, w_up, w_down, fgain, *, nb, t):
    bsz, seq, _ = x1.shape
    grid = (bsz // nb, seq // t)
    kern = functools.partial(_mlp_kernel, nb=nb, t=t)
    seq_spec = pl.BlockSpec((nb, t, D_MODEL), lambda b, i: (b, i, 0))
    return pl.pallas_call(
        kern,
        grid=grid,
        in_specs=[
            seq_spec,
            pl.BlockSpec((nb, 1, N_MOD * D_MODEL), lambda b, i: (b, 0, 0)),
            _const_spec((D_MODEL, D_FF)),
            _const_spec((D_FF, D_MODEL)),
            _const_spec((1, D_MODEL)),
        ],
        out_specs=seq_spec,
        out_shape=jax.ShapeDtypeStruct((bsz, seq, D_MODEL), F32),
        scratch_shapes=[pltpu.VMEM((nb * t, D_MODEL), BF16)],
        compiler_params=pltpu.CompilerParams(
            dimension_semantics=("arbitrary", "arbitrary"), vmem_limit_bytes=VMEM_LIMIT),
        name="mlp",
    )(x1, mod, w_up, w_down, fgain)


def _block_diag(w):
    n, c, d = w.shape
    eye = jnp.eye(n, dtype=w.dtype)
    return (eye[:, None, :, None] * w[:, :, None, :]).reshape(n * c, n * d)


def kernel(x_prompt, x_sample, c_prompt, c_sample, state_hgrn, state_rglru, cache_conv,
           hg_lb_logits, w_ada, b_ada, w_in, hg_norm_gain, conv_w, conv_b,
           rg_wa, rg_ba, rg_wx, rg_bx, rg_lambda, w_out, w_up, w_down, final_gain):
    bp = x_prompt.shape[0]
    bs = x_sample.shape[0]

    w_in_b = w_in[0].astype(BF16)
    w_out_b = w_out[0].astype(BF16)
    w_up_b = w_up[0].astype(BF16)
    w_down_b = w_down[0].astype(BF16)
    per = MXU_DIM // rg_wa.shape[2]
    w_rg = jnp.stack([
        jnp.concatenate([_block_diag(rg_wa[0][p * per:(p + 1) * per]),
                         _block_diag(rg_wx[0][p * per:(p + 1) * per])], axis=1)
        for p in range(RG_WIDTH // MXU_DIM)]).astype(BF16)
    b_rg = jnp.concatenate([rg_ba[0], rg_bx[0]])[None, :]
    gain = hg_norm_gain[0][None, :]
    cbias = conv_b[0][None, :]
    lam = rg_lambda[0][None, :]
    fgain = final_gain[None, :]

    c_all = jnp.concatenate([c_prompt, c_sample], axis=0)
    mod = _ada(c_all, w_ada[0], b_ada[0][None, :])
    mod_p = mod[:bp][:, None, :]
    mod_s = mod[bp:][:, None, :]

    def trunk(x, mod_x, s0, h0, cb0):
        nb, t = _block_rows(x.shape[0], x.shape[1])
        x1, s_new, h_new, cb_new = _mixer(
            x, mod_x, s0, h0[:, None, :], cb0, hg_lb_logits, w_in_b, gain, conv_w[0], cbias,
            w_rg, b_rg, lam, w_out_b, nb=nb, t=t)
        y = _mlp(x1, mod_x, w_up_b, w_down_b, fgain, nb=nb, t=t)
        return y, s_new[None], h_new[:, 0, :][None], cb_new[None]

    s0p = jnp.zeros((bp, N_HEADS, HEAD_DIM, HEAD_DIM), state_hgrn.dtype)
    h0p = jnp.zeros((bp, RG_WIDTH), state_rglru.dtype)
    cbp = jnp.zeros((bp, CONV_WIDTH - 1, RG_WIDTH), cache_conv.dtype)
    y_p, s_p, h_p, cb_p = trunk(x_prompt, mod_p, s0p, h0p, cbp)
    y_s, s_s, h_s, cb_s = trunk(x_sample, mod_s, state_hgrn[0], state_rglru[0], cache_conv[0])
    return (y_p, y_s, s_p, h_p, cb_p, s_s, h_s, cb_s)
```
